```python
import math
import jax, jax.numpy as jnp
from jax import lax
import numpy as np

D_MODEL = 1024
BATCH = 2
SEQ = 8192
DEPTH = 2
DEC_BATCH = 2
DEC_SEQ = 16384
PAST_LEN = 128

BLOCK = 128
BRANCH_WIDTH = D_MODEL // 2
N_BRANCHES = 4
RMS_EPS = 1e-6
RET_HEADS = 4
RET_DK = D_MODEL // 16
RET_DV = BRANCH_WIDTH // RET_HEADS
RET_QK_WIDTH = RET_HEADS * RET_DK
ROPE_BASE = 10000.0
SSD_HEADDIM = 64
SSD_HEADS = BRANCH_WIDTH // SSD_HEADDIM
SSD_GROUPS = 2
SSD_STATE = 64
SSD_CONV_K = 5
SSD_CONV_DIM = BRANCH_WIDTH + 2 * SSD_GROUPS * SSD_STATE
SWA_HEADS = 8
SWA_KV_HEADS = 2
SWA_HEAD_DIM = BRANCH_WIDTH // SWA_HEADS
WINDOW = 128
DIFF_HEADS = 4
DIFF_DK = 64
DIFF_DV = BRANCH_WIDTH // DIFF_HEADS
DIFF_QK_WIDTH = DIFF_HEADS * 2 * DIFF_DK
NUM_BUCKETS = 32
MAX_DISTANCE = 128
ATTN_BIAS_HEADS = SWA_HEADS + DIFF_HEADS
FFN_DENSE = ((8 * D_MODEL // 3 + 127) // 128) * 128
N_EXPERTS = 8
TOP_K = 2
FFN_EXPERT = 7 * D_MODEL // 2
MOE_BLOCK = 128
N_DENSE_LAYERS = (DEPTH + 1) // 2
N_MOE_LAYERS = DEPTH // 2
IN_SPLITS = [RET_QK_WIDTH, RET_QK_WIDTH, BRANCH_WIDTH, BRANCH_WIDTH,
             BRANCH_WIDTH, SSD_CONV_DIM, 2 * SSD_HEADS,
             BRANCH_WIDTH, SWA_KV_HEADS * SWA_HEAD_DIM, SWA_KV_HEADS * SWA_HEAD_DIM,
             DIFF_QK_WIDTH, DIFF_QK_WIDTH, BRANCH_WIDTH]
IN_WIDTH = sum(IN_SPLITS)
IN_OFFSETS = [sum(IN_SPLITS[:i]) for i in range(1, len(IN_SPLITS))]

kernel_name = 'hybrid_bidir_encoder_retention_ssd_swa_diffattn_moe'


def rms_norm(x, w):
    xf = x.astype(jnp.float32)
    y = xf * lax.rsqrt(jnp.mean(xf * xf, axis=-1, keepdims=True) + RMS_EPS)
    return (y * w.astype(jnp.float32)).astype(x.dtype)


def modulate(x, norm_w, shift, scale):
    return rms_norm(x, norm_w) * (1 + scale[:, None, :]) + shift[:, None, :]


def rope(x, positions):
    half = x.shape[-1] // 2
    inv = ROPE_BASE ** (-jnp.arange(half, dtype=jnp.float32) / half)
    ang = positions.astype(jnp.float32)[:, None] * inv[None, :]
    cos = jnp.cos(ang)[None, :, None, :]
    sin = jnp.sin(ang)[None, :, None, :]
    x1 = x[..., :half].astype(jnp.float32)
    x2 = x[..., half:].astype(jnp.float32)
    return jnp.concatenate([x1 * cos - x2 * sin, x1 * sin + x2 * cos], axis=-1).astype(x.dtype)


def t5_bucket(rel):
    half = NUM_BUCKETS // 2
    max_exact = half // 2
    n = jnp.abs(rel)
    sign = jnp.where(rel > 0, half, 0)
    large = max_exact + (jnp.log(jnp.maximum(n, 1).astype(jnp.float32) / max_exact)
                         / math.log(MAX_DISTANCE / max_exact) * (half - max_exact)).astype(jnp.int32)
    large = jnp.minimum(large, half - 1)
    return sign + jnp.where(n < max_exact, n, large)


def chunked_decay_attention(q, k, v, log_a, include_diag):
    b, L, h, dk = q.shape
    dv = v.shape[-1]
    nc = L // BLOCK
    qc = q.reshape(b, nc, BLOCK, h, dk)
    kc = k.reshape(b, nc, BLOCK, h, dk)
    vc = v.reshape(b, nc, BLOCK, h, dv)
    acum = jnp.cumsum(log_a.astype(jnp.float32).reshape(b, nc, BLOCK, h), axis=2)
    idx = jnp.arange(BLOCK)
    mask = (idx[:, None] >= idx[None, :]) if include_diag else (idx[:, None] > idx[None, :])
    acum_h = jnp.moveaxis(acum, 3, 2)
    seg = acum_h[..., :, None] - acum_h[..., None, :]
    decay = jnp.exp(jnp.where(mask, seg, -jnp.inf))
    scores = jnp.einsum('bcihd,bcjhd->bchij', qc, kc) * decay
    y_intra = jnp.einsum('bchij,bcjhe->bcihe', scores, vc)
    a_last = acum[:, :, -1, :]
    to_end = jnp.exp(a_last[:, :, None, :] - acum)
    chunk_states = jnp.einsum('bcjh,bcjhd,bcjhe->bchde', to_end, kc, vc)
    chunk_decay = jnp.exp(a_last)

    def step(state, inp):
        dec, new = inp
        return dec[:, :, None, None] * state + new, state

    init = jnp.zeros((b, h, dk, dv), jnp.float32)
    _, prev = lax.scan(step, init, (jnp.moveaxis(chunk_decay, 1, 0), jnp.moveaxis(chunk_states, 1, 0)))
    prev = jnp.moveaxis(prev, 0, 1)
    y_cross = jnp.einsum('bcihd,bchde->bcihe', qc * jnp.exp(acum)[..., None], prev)
    return (y_intra + y_cross).reshape(b, L, h, dv)


def retention_branch(q, k, v, g, decay_logit, norm_w, positions):
    b, L, _ = q.shape
    q = rope(q.reshape(b, L, RET_HEADS, RET_DK), positions)
    k = rope(k.reshape(b, L, RET_HEADS, RET_DK), positions) * (RET_DK ** -0.5)
    v = v.reshape(b, L, RET_HEADS, RET_DV)
    log_gamma = jax.nn.log_sigmoid(decay_logit.astype(jnp.float32))
    la_f = jnp.broadcast_to(log_gamma[0], (b, L, RET_HEADS))
    la_b = jnp.broadcast_to(log_gamma[1], (b, L, RET_HEADS))
    y_f = chunked_decay_attention(q, k, v, la_f, True)
    y_b = jnp.flip(chunked_decay_attention(jnp.flip(q, 1), jnp.flip(k, 1), jnp.flip(v, 1), la_b, False), 1)
    y = (y_f + y_b).astype(jnp.float32)
    mu = jnp.mean(y, axis=-1, keepdims=True)
    var = jnp.mean(jnp.square(y - mu), axis=-1, keepdims=True)
    y = ((y - mu) * lax.rsqrt(var + RMS_EPS)).reshape(b, L, BRANCH_WIDTH) * norm_w.astype(jnp.float32)
    return (jax.nn.silu(g.astype(jnp.float32)) * y).astype(g.dtype)


def ssd_branch(z, xbc, dt, conv_w, conv_b, dt_bias, a_log, d_skip, norm_w):
    b, L, _ = xbc.shape
    xbc = lax.conv_general_dilated(xbc, conv_w[:, None, :].astype(xbc.dtype), window_strides=(1,),
                                   padding=[(SSD_CONV_K // 2, SSD_CONV_K // 2)],
                                   dimension_numbers=('NWC', 'WIO', 'NWC'),
                                   feature_group_count=SSD_CONV_DIM)
    xbc = jax.nn.silu(xbc + conv_b)
    xs, bmat, cmat = jnp.split(xbc, [BRANCH_WIDTH, BRANCH_WIDTH + SSD_GROUPS * SSD_STATE], axis=-1)
    xs = xs.reshape(b, L, SSD_HEADS, SSD_HEADDIM)
    rep = SSD_HEADS // SSD_GROUPS
    bmat = jnp.repeat(bmat.reshape(b, L, SSD_GROUPS, SSD_STATE), rep, axis=2)
    cmat = jnp.repeat(cmat.reshape(b, L, SSD_GROUPS, SSD_STATE), rep, axis=2)
    dt = jax.nn.softplus(dt.astype(jnp.float32).reshape(b, L, 2, SSD_HEADS) + dt_bias.astype(jnp.float32))
    a = -jnp.exp(a_log.astype(jnp.float32))
    dt_f, dt_b = dt[:, :, 0], dt[:, :, 1]
    y_f = chunked_decay_attention(cmat, bmat, xs * dt_f[..., None], dt_f * a[0], True)
    y_b = jnp.flip(chunked_decay_attention(jnp.flip(cmat, 1), jnp.flip(bmat, 1),
                                           jnp.flip(xs * dt_b[..., None], 1),
                                           jnp.flip(dt_b * a[1], 1), True), 1)
    y = y_f + y_b + xs * d_skip[:, None]
    y = y.reshape(b, L, BRANCH_WIDTH) * jax.nn.silu(z.astype(jnp.float32))
    yg = y.reshape(b, L, SSD_GROUPS, BRANCH_WIDTH // SSD_GROUPS)
    yg = yg * lax.rsqrt(jnp.mean(yg * yg, axis=-1, keepdims=True) + RMS_EPS)
    return (yg.reshape(b, L, BRANCH_WIDTH) * norm_w.astype(jnp.float32)).astype(z.dtype)


def swa_branch(q, k, v, sink, bias_table):
    b, L, _ = q.shape
    nb = L // BLOCK
    grp = SWA_HEADS // SWA_KV_HEADS
    qb = q.reshape(b, nb, BLOCK, SWA_KV_HEADS, grp, SWA_HEAD_DIM) * (SWA_HEAD_DIM ** -0.5)

    def band(t):
        tp = jnp.pad(t.reshape(b, L, SWA_KV_HEADS, SWA_HEAD_DIM), ((0, 0), (BLOCK, BLOCK), (0, 0), (0, 0)))
        tp = tp.reshape(b, nb + 2, BLOCK, SWA_KV_HEADS, SWA_HEAD_DIM)
        return jnp.concatenate([tp[:, :-2], tp[:, 1:-1], tp[:, 2:]], axis=2)

    kb, vb = band(k), band(v)
    s = jnp.einsum('bnqhgd,bnkhd->bnhgqk', qb, kb).astype(jnp.float32)
    krel = jnp.arange(3 * BLOCK) - BLOCK
    rel = krel[None, :] - jnp.arange(BLOCK)[:, None]
    bias = jnp.transpose(bias_table[t5_bucket(rel)], (2, 0, 1)).reshape(SWA_KV_HEADS, grp, BLOCK, 3 * BLOCK)
    kabs = jnp.arange(nb)[:, None] * BLOCK + krel[None, :]
    valid = (jnp.abs(rel) <= WINDOW)[None] & ((kabs >= 0) & (kabs < L))[:, None, :]
    s = jnp.where(valid[None, :, None, None], s + bias.astype(jnp.float32), -jnp.inf)
    sink_logit = jnp.broadcast_to(sink.astype(jnp.float32).reshape(SWA_KV_HEADS, grp)[None, None, :, :, None, None],
                                  s.shape[:-1] + (1,))
    p = jax.nn.softmax(jnp.concatenate([s, sink_logit], axis=-1), axis=-1)[..., :-1]
    o = jnp.einsum('bnhgqk,bnkhd->bnqhgd', p.astype(vb.dtype), vb)
    return o.reshape(b, L, BRANCH_WIDTH)


def diff_branch(q, k, v, lam_params, lam_init, norm_w, bias_table):
    b, L, _ = q.shape
    nb = L // BLOCK
    qb = q.reshape(b, nb, BLOCK, DIFF_HEADS, 2, DIFF_DK) * (DIFF_DK ** -0.5)
    k = k.reshape(b, L, DIFF_HEADS, 2, DIFF_DK)
    v = v.reshape(b, L, DIFF_HEADS, DIFF_DV)
    lp = lam_params.astype(jnp.float32)
    lam = jnp.exp(jnp.sum(lp[0] * lp[1])) - jnp.exp(jnp.sum(lp[2] * lp[3])) + lam_init
    kpos = jnp.arange(L)

    def one_block(args):
        qblk, blk = args
        s = jnp.einsum('bqhcd,bkhcd->bhcqk', qblk, k).astype(jnp.float32)
        qpos = blk * BLOCK + jnp.arange(BLOCK)
        bias = jnp.transpose(bias_table[t5_bucket(kpos[None, :] - qpos[:, None])], (2, 0, 1))
        p = jax.nn.softmax(s + bias[None, :, None].astype(jnp.float32), axis=-1)
        attn = p[:, :, 0] - lam * p[:, :, 1]
        return jnp.einsum('bhqk,bkhe->bqhe', attn.astype(v.dtype), v)

    out = lax.map(one_block, (jnp.moveaxis(qb, 1, 0), jnp.arange(nb)))
    out = jnp.moveaxis(out, 0, 1).reshape(b, L, DIFF_HEADS, DIFF_DV).astype(jnp.float32)
    out = out * lax.rsqrt(jnp.mean(out * out, axis=-1, keepdims=True) + RMS_EPS)
    return (out.reshape(b, L, BRANCH_WIDTH) * norm_w.astype(jnp.float32) * (1 - lam_init)).astype(v.dtype)


def mixer_sublayer(h, p, l):
    b, L, _ = h.shape
    positions = jnp.arange(L)
    u = h @ p['w_in'][l]
    rq, rk, rv, rg, sz, sxbc, sdt, cq, ck, cv, dq, dk_, dv_ = jnp.split(u, IN_OFFSETS, axis=-1)
    y_a = retention_branch(rq, rk, rv, rg, p['ret_decay_logit'][l], p['ret_norm_w'][l], positions)
    y_b = ssd_branch(sz, sxbc, sdt, p['ssd_conv_w'][l], p['ssd_conv_b'][l], p['ssd_dt_bias'][l],
                     p['ssd_a_log'][l], p['ssd_d'][l], p['ssd_norm_w'][l])
    y_c = swa_branch(cq, ck, cv, p['swa_sink'][l], p['rel_bias'][:, :SWA_HEADS])
    y_d = diff_branch(dq, dk_, dv_, p['diff_lambda'][l], 0.8 - 0.6 * math.exp(-0.3 * l),
                      p['diff_norm_w'][l], p['rel_bias'][:, SWA_HEADS:])
    gates = jax.nn.sigmoid((h @ p['w_gate'][l] + p['b_gate'][l]).astype(jnp.float32))
    gates = gates.reshape(b, L, N_BRANCHES, D_MODEL)
    branches = jnp.stack([y_a, y_b, y_c, y_d], axis=2)
    proj = jnp.einsum('blnw,nwd->blnd', branches, p['w_branch'][l])
    merged = jnp.sum(gates * proj, axis=2).astype(h.dtype)
    return merged @ p['w_out'][l]


def dense_swiglu(h, w1, w3, w2):
    return (jax.nn.silu(h @ w1) * (h @ w3)) @ w2


def moe_swiglu(h, router, w1, w3, w2):
    b, L, d = h.shape
    t = b * L
    xf = h.reshape(t, d)
    logits = (xf @ router).astype(jnp.float32)
    top_val, top_idx = lax.top_k(logits, TOP_K)
    gates = jax.nn.softmax(top_val, axis=-1)
    n_assign = t * TOP_K
    flat_e = top_idx.reshape(-1)
    flat_tok = jnp.repeat(jnp.arange(t, dtype=jnp.int32), TOP_K)
    flat_g = gates.reshape(-1)
    order = jnp.argsort(flat_e)
    se, stok, sg = flat_e[order], flat_tok[order], flat_g[order]
    counts = jnp.bincount(flat_e, length=N_EXPERTS)
    starts = jnp.cumsum(counts) - counts
    padded = (counts + MOE_BLOCK - 1) // MOE_BLOCK * MOE_BLOCK
    pad_ends = jnp.cumsum(padded)
    pad_starts = pad_ends - padded
    dest = pad_starts[se] + jnp.arange(n_assign, dtype=jnp.int32) - starts[se]
    n_blocks = -(-n_assign // MOE_BLOCK) + N_EXPERTS
    n_rows = n_blocks * MOE_BLOCK
    row_tok = jnp.full((n_rows,), t, jnp.int32).at[dest].set(stok)
    row_gate = jnp.zeros((n_rows,), jnp.float32).at[dest].set(sg)
    block_start = jnp.arange(n_blocks, dtype=jnp.int32) * MOE_BLOCK
    block_expert = jnp.minimum(jnp.sum(pad_ends[None, :] <= block_start[:, None], axis=1), N_EXPERTS - 1)
    x_rows = jnp.concatenate([xf, jnp.zeros((1, d), xf.dtype)], axis=0)[row_tok].reshape(n_blocks, MOE_BLOCK, d)

    def expert_block(args):
        xb, e = args
        return (jax.nn.silu(xb @ w1[e]) * (xb @ w3[e])) @ w2[e]

    y_rows = lax.map(expert_block, (x_rows, block_expert)).reshape(n_rows, d)
    y = jnp.zeros((t + 1, d), jnp.float32).at[row_tok].add(y_rows.astype(jnp.float32) * row_gate[:, None])
    return y[:t].reshape(b, L, d).astype(h.dtype)


def trunk(x, c, p):
    for l in range(DEPTH):
        mod = jax.nn.silu(c) @ p['w_ada'][l] + p['b_ada'][l]
        sh1, sc1, g1, sh2, sc2, g2 = jnp.split(mod, 6, axis=-1)
        h = modulate(x, p['norm_mix_w'][l], sh1, sc1)
        x = x + g1[:, None, :] * mixer_sublayer(h, p, l)
        h = modulate(x, p['norm_ffn_w'][l], sh2, sc2)
        i = l // 2
        if l % 2 == 0:
            f = dense_swiglu(h, p['ffn_w1'][i], p['ffn_w3'][i], p['ffn_w2'][i])
        else:
            f = moe_swiglu(h, p['moe_router'][i], p['moe_w1'][i], p['moe_w3'][i], p['moe_w2'][i])
        x = x + g2[:, None, :] * f
    return rms_norm(x, p['final_norm_w'])


def setup_inputs(seed: int = 0) -> dict:
    key = jax.random.key(seed)
    ks = iter(jax.random.split(key, 48))
    f32 = jnp.float32

    def nrm(shape, scale):
        return scale * jax.random.normal(next(ks), shape, f32)

    d = D_MODEL
    inp = {}
    inp['x_prompt'] = nrm((BATCH, SEQ, d), 1.0)
    inp['x_sample'] = nrm((DEC_BATCH, DEC_SEQ, d), 1.0)
    inp['c_prompt'] = nrm((BATCH, d), 1.0)
    inp['c_sample'] = nrm((DEC_BATCH, d), 1.0)
    inp['norm_mix_w'] = 1.0 + nrm((DEPTH, d), 0.02)
    inp['norm_ffn_w'] = 1.0 + nrm((DEPTH, d), 0.02)
    inp['w_ada'] = nrm((DEPTH, d, 6 * d), 0.5 * d ** -0.5)
    inp['b_ada'] = nrm((DEPTH, 6 * d), 0.02)
    inp['w_in'] = nrm((DEPTH, d, IN_WIDTH), d ** -0.5)
    base_logit = jnp.log(2.0 ** (5.0 + jnp.arange(RET_HEADS, dtype=f32)) - 1.0)
    inp['ret_decay_logit'] = base_logit + nrm((DEPTH, 2, RET_HEADS), 0.1)
    inp['ret_norm_w'] = 1.0 + nrm((DEPTH, BRANCH_WIDTH), 0.02)
    inp['ssd_conv_w'] = nrm((DEPTH, SSD_CONV_K, SSD_CONV_DIM), SSD_CONV_K ** -0.5)
    inp['ssd_conv_b'] = nrm((DEPTH, SSD_CONV_DIM), 0.02)
    dt0 = jnp.exp(jax.random.uniform(next(ks), (DEPTH, 2, SSD_HEADS), f32,
                                     minval=math.log(1e-3), maxval=math.log(1e-1)))
    inp['ssd_dt_bias'] = dt0 + jnp.log(-jnp.expm1(-dt0))
    inp['ssd_a_log'] = jnp.log(jax.random.uniform(next(ks), (DEPTH, 2, SSD_HEADS), f32, minval=1.0, maxval=16.0))
    inp['ssd_d'] = 1.0 + nrm((DEPTH, SSD_HEADS), 0.1)
    inp['ssd_norm_w'] = 1.0 + nrm((DEPTH, BRANCH_WIDTH), 0.02)
    inp['swa_sink'] = nrm((DEPTH, SWA_HEADS), 0.5)
    inp['diff_lambda'] = nrm((DEPTH, 4, DIFF_DK), 0.1)
    inp['diff_norm_w'] = 1.0 + nrm((DEPTH, BRANCH_WIDTH), 0.02)
    inp['rel_bias'] = nrm((NUM_BUCKETS, ATTN_BIAS_HEADS), 0.2)
    inp['w_gate'] = nrm((DEPTH, d, N_BRANCHES * d), d ** -0.5)
    inp['b_gate'] = nrm((DEPTH, N_BRANCHES * d), 0.02)
    inp['w_branch'] = nrm((DEPTH, N_BRANCHES, BRANCH_WIDTH, d), BRANCH_WIDTH ** -0.5)
    inp['w_out'] = nrm((DEPTH, d, d), d ** -0.5)
    inp['ffn_w1'] = nrm((N_DENSE_LAYERS, d, FFN_DENSE), d ** -0.5)
    inp['ffn_w3'] = nrm((N_DENSE_LAYERS, d, FFN_DENSE), d ** -0.5)
    inp['ffn_w2'] = nrm((N_DENSE_LAYERS, FFN_DENSE, d), FFN_DENSE ** -0.5)
    inp['moe_router'] = nrm((N_MOE_LAYERS, d, N_EXPERTS), d ** -0.5)
    inp['moe_w1'] = nrm((N_MOE_LAYERS, N_EXPERTS, d, FFN_EXPERT), d ** -0.5)
    inp['moe_w3'] = nrm((N_MOE_LAYERS, N_EXPERTS, d, FFN_EXPERT), d ** -0.5)
    inp['moe_w2'] = nrm((N_MOE_LAYERS, N_EXPERTS, FFN_EXPERT, d), FFN_EXPERT ** -0.5)
    inp['final_norm_w'] = 1.0 + nrm((d,), 0.02)
    return inp


def reference(x_prompt, x_sample, c_prompt, c_sample, norm_mix_w, norm_ffn_w, w_ada, b_ada, w_in,
              ret_decay_logit, ret_norm_w, ssd_conv_w, ssd_conv_b, ssd_dt_bias, ssd_a_log, ssd_d,
              ssd_norm_w, swa_sink, diff_lambda, diff_norm_w, rel_bias, w_gate, b_gate, w_branch, w_out,
              ffn_w1, ffn_w3, ffn_w2, moe_router, moe_w1, moe_w3, moe_w2, final_norm_w):
    p = {'norm_mix_w': norm_mix_w, 'norm_ffn_w': norm_ffn_w, 'w_ada': w_ada, 'b_ada': b_ada,
         'w_in': w_in, 'ret_decay_logit': ret_decay_logit, 'ret_norm_w': ret_norm_w,
         'ssd_conv_w': ssd_conv_w, 'ssd_conv_b': ssd_conv_b, 'ssd_dt_bias': ssd_dt_bias,
         'ssd_a_log': ssd_a_log, 'ssd_d': ssd_d, 'ssd_norm_w': ssd_norm_w, 'swa_sink': swa_sink,
         'diff_lambda': diff_lambda, 'diff_norm_w': diff_norm_w, 'rel_bias': rel_bias,
         'w_gate': w_gate, 'b_gate': b_gate, 'w_branch': w_branch, 'w_out': w_out,
         'ffn_w1': ffn_w1, 'ffn_w3': ffn_w3, 'ffn_w2': ffn_w2, 'moe_router': moe_router,
         'moe_w1': moe_w1, 'moe_w3': moe_w3, 'moe_w2': moe_w2, 'final_norm_w': final_norm_w}
    y_prompt = trunk(x_prompt, c_prompt, p)
    y_sample = trunk(x_sample, c_sample, p)
    return (y_prompt, y_sample)
```

```python
import functools
import math

import numpy as np
import jax
import jax.numpy as jnp
from jax import lax
from jax.experimental import pallas as pl
from jax.experimental.pallas import tpu as pltpu

F32 = jnp.float32
BF16 = jnp.bfloat16

D_MODEL = 1024
DEPTH = 2
BRANCH_WIDTH = 512
RMS_EPS = 1e-6
RET_HEADS, RET_DK, RET_DV = 4, 64, 128
ROPE_BASE = 10000.0
SSD_HEADS, SSD_HEADDIM, SSD_GROUPS, SSD_STATE, SSD_CONV_K = 8, 64, 2, 64, 5
SWA_HEADS, SWA_KV_HEADS, SWA_HEAD_DIM, WINDOW = 8, 2, 64, 128
DIFF_HEADS, DIFF_DK, DIFF_DV = 4, 64, 128
NUM_BUCKETS, MAX_DISTANCE = 32, 128
FFN_DENSE = 2816
N_EXPERTS, TOP_K, FFN_EXPERT = 8, 2, 3584
LANES = 128
NEG_BIG = -1e30

U_RQ, U_RK, U_RV, U_RG = 0, 256, 512, 1024
U_SZ, U_SX = 1536, 2048
U_CQ, U_DQ, U_DK, U_DV = 2560, 3072, 3584, 4096
U_SBC, U_CK, U_CV = 4608, 4864, 4992
U_WIDTH = 5120

RET_T = 256
SSD_T = 128
SWA_T = 128
DIFF_T = 512
VMEM_LIMIT = 56 * 1024 * 1024


def _cparams(sem):
    return pltpu.CompilerParams(dimension_semantics=sem, vmem_limit_bytes=VMEM_LIMIT)


def _dot(a, b):
    return jnp.dot(a, b, preferred_element_type=F32)


def _dot_nt(a, b):
    return lax.dot_general(a, b, (((1,), (1,)), ((), ())), preferred_element_type=F32)


def _dot_tn(a, b):
    return lax.dot_general(a, b, (((0,), (0,)), ((), ())), preferred_element_type=F32)


def _silu(x):
    return x * jax.nn.sigmoid(x)


def _split2(x):
    hi = x.astype(BF16)
    lo = (x - hi.astype(F32)).astype(BF16)
    return hi, lo


def _split3(x):
    hi = x.astype(BF16)
    r = x - hi.astype(F32)
    mid = r.astype(BF16)
    lo = (r - mid.astype(F32)).astype(BF16)
    return hi, mid, lo


def _modulate(x, nw, shift, scale):
    y = x * lax.rsqrt(jnp.mean(x * x, axis=-1, keepdims=True) + RMS_EPS) * nw
    return y * (1.0 + scale) + shift


def _ada_kernel(c_ref, w_ref, b_ref, o_ref):
    s = _silu(c_ref[...]).astype(BF16)
    o_ref[0] = _dot(s, w_ref[0].astype(BF16)) + b_ref[0]


def _ada(c, w_ada, b_ada):
    b = c.shape[0]
    cp = jnp.zeros((8, D_MODEL), F32).at[:b].set(c)
    tn = 1536
    out = pl.pallas_call(
        _ada_kernel,
        grid=(DEPTH, 6 * D_MODEL // tn),
        in_specs=[pl.BlockSpec((8, D_MODEL), lambda l, j: (0, 0)),
                  pl.BlockSpec((1, D_MODEL, tn), lambda l, j: (l, 0, j)),
                  pl.BlockSpec((1, 1, tn), lambda l, j: (l, 0, j))],
        out_specs=pl.BlockSpec((1, 8, tn), lambda l, j: (l, 0, j)),
        out_shape=jax.ShapeDtypeStruct((DEPTH, 8, 6 * D_MODEL), F32),
        compiler_params=_cparams(("arbitrary", "arbitrary")),
        name="ada",
    )(cp, w_ada, b_ada.reshape(DEPTH, 1, 6 * D_MODEL))
    return out[:, :b].reshape(DEPTH, b, 6, D_MODEL)


def _inproj_kernel(x_ref, mod_ref, nw_ref, w_ref, wdt_ref, u_ref, dt_ref, h_sc):
    @pl.when(pl.program_id(1) == 0)
    def _():
        h = _modulate(x_ref[...], nw_ref[...], mod_ref[0, 0:1, :], mod_ref[0, 1:2, :]).astype(BF16)
        h_sc[...] = h
        dt_ref[...] = _dot(h, wdt_ref[...])

    u_ref[...] = _dot(h_sc[...], w_ref[...]).astype(BF16)


def _inproj(x2, mod3, nw, w_u, w_dt, L):
    m = x2.shape[0]
    tm, tn = 1024, 512
    per_seq = L // tm
    return pl.pallas_call(
        _inproj_kernel,
        grid=(m // tm, U_WIDTH // tn),
        in_specs=[pl.BlockSpec((tm, D_MODEL), lambda i, j: (i, 0)),
                  pl.BlockSpec((1, 6, D_MODEL), lambda i, j: (i // per_seq, 0, 0)),
                  pl.BlockSpec((1, D_MODEL), lambda i, j: (0, 0)),
                  pl.BlockSpec((D_MODEL, tn), lambda i, j: (0, j)),
                  pl.BlockSpec((D_MODEL, LANES), lambda i, j: (0, 0))],
        out_specs=[pl.BlockSpec((tm, tn), lambda i, j: (i, j)),
                   pl.BlockSpec((tm, LANES), lambda i, j: (i, 0))],
        out_shape=[jax.ShapeDtypeStruct((m, U_WIDTH), BF16),
                   jax.ShapeDtypeStruct((m, LANES), F32)],
        scratch_shapes=[pltpu.VMEM((tm, D_MODEL), BF16)],
        compiler_params=_cparams(("arbitrary", "arbitrary")),
        name="inproj",
    )(x2, mod3, nw, w_u, w_dt)


def _ret_kernel(q_ref, k_ref, v_ref, g_ref, cos_ref, sin_ref, dbi_ref, gq_ref, gk_ref, dec_ref,
                nw_ref, o_ref, sf_sc, sb_sc, sbst_sc, *, nc):
    T = RET_T
    p = pl.program_id(1)
    c = pl.program_id(2)
    cos = cos_ref[...]
    sin = sin_ref[...]
    lane = lax.broadcasted_iota(jnp.int32, (T, 2 * LANES), 1)
    first_half = (lane % RET_DK) < (RET_DK // 2)
    lo_head = lax.broadcasted_iota(jnp.int32, (T, LANES), 1) < RET_DK

    def rope(ref):
        x = ref[...].astype(F32)
        rot = jnp.where(first_half, pltpu.roll(x, 2 * LANES - RET_DK // 2, 1),
                        pltpu.roll(x, RET_DK // 2, 1))
        return x * cos + rot * sin

    def head_masked(x, h):
        t = x[:, (h // 2) * LANES:(h // 2 + 1) * LANES]
        keep = lo_head if h % 2 == 0 else jnp.logical_not(lo_head)
        return jnp.where(keep, t, jnp.zeros_like(t))

    @pl.when(p == 0)
    def _():
        @pl.when(c == 0)
        def _():
            sb_sc[...] = jnp.zeros_like(sb_sc)

        kb = (rope(k_ref) * (RET_DK ** -0.5) * gk_ref[1]).astype(BF16)
        v = v_ref[...]
        cc = nc - 1 - c
        for t in range(RET_HEADS // 2):
            sb = sb_sc[t]
            sbst_sc[cc, t] = sb.astype(BF16)
            upd = (_dot_tn(head_masked(kb, 2 * t), v[:, (2 * t) * RET_DV:(2 * t + 1) * RET_DV])
                   + _dot_tn(head_masked(kb, 2 * t + 1), v[:, (2 * t + 1) * RET_DV:(2 * t + 2) * RET_DV]))
            sb_sc[t] = sb * dec_ref[1, t] + upd

    @pl.when(p == 1)
    def _():
        @pl.when(c == 0)
        def _():
            sf_sc[...] = jnp.zeros_like(sf_sc)

        qr = rope(q_ref)
        kr = rope(k_ref) * (RET_DK ** -0.5)
        qb = qr.astype(BF16)
        kb = kr.astype(BF16)
        qf = (qr * gq_ref[0]).astype(BF16)
        qw = (qr * gq_ref[1]).astype(BF16)
        kf = (kr * gk_ref[0]).astype(BF16)
        v = v_ref[...]
        ys = []
        upds = []
        for h in range(RET_HEADS):
            t = h // 2
            vh = v[:, h * RET_DV:(h + 1) * RET_DV]
            kt = kb[:, t * LANES:(t + 1) * LANES]
            s = _dot_nt(head_masked(qb, h), kt)
            y = _dot((s * dbi_ref[h]).astype(BF16), vh)
            y = y + _dot(head_masked(qf, h), sf_sc[t].astype(BF16))
            y = y + _dot(head_masked(qw, h), sbst_sc[c, t])
            upds.append(_dot_tn(head_masked(kf, h), vh))
            mu = jnp.mean(y, axis=-1, keepdims=True)
            yc = y - mu
            var = jnp.mean(yc * yc, axis=-1, keepdims=True)
            ys.append(yc * lax.rsqrt(var + RMS_EPS))
        for t in range(RET_HEADS // 2):
            sf_sc[t] = sf_sc[t] * dec_ref[0, t] + upds[2 * t] + upds[2 * t + 1]
        y = jnp.concatenate(ys, axis=1) * nw_ref[...]
        g = g_ref[...].astype(F32)
        o_ref[...] = (_silu(g) * y).astype(BF16)


def _ret_tables(decay_logit):
    T = RET_T
    lg = jax.nn.log_sigmoid(decay_logit.astype(F32))
    i = jnp.arange(T, dtype=F32)
    diff = i[:, None] - i[None, :]
    dbi = jnp.where(diff[None] >= 0, jnp.exp(lg[0][:, None, None] * diff[None]),
                    jnp.exp(-lg[1][:, None, None] * diff[None]))

    def cols(per_head):
        return jnp.repeat(per_head.T, RET_DK, axis=1)

    gq = jnp.stack([cols(jnp.exp(lg[0][:, None] * (i + 1)[None])),
                    cols(jnp.exp(lg[1][:, None] * (T - i)[None]))])
    gk = jnp.stack([cols(jnp.exp(lg[0][:, None] * (T - 1 - i)[None])),
                    cols(jnp.exp(lg[1][:, None] * i[None]))])
    dec = jnp.exp(lg * T)
    dec = jnp.repeat(dec.reshape(2, RET_HEADS // 2, 2, 1), RET_DK, axis=2).reshape(2, RET_HEADS // 2, LANES, 1)
    dec = jnp.broadcast_to(dec, (2, RET_HEADS // 2, LANES, RET_DV))
    return dbi, gq, gk, dec


def _rope_tables(L):
    half = RET_DK // 2
    inv = ROPE_BASE ** (-jnp.arange(half, dtype=F32) / half)
    ang = jnp.arange(L, dtype=F32)[:, None] * inv[None, :]
    cos = jnp.tile(jnp.concatenate([jnp.cos(ang), jnp.cos(ang)], axis=1), (1, RET_HEADS))
    sin = jnp.tile(jnp.concatenate([-jnp.sin(ang), jnp.sin(ang)], axis=1), (1, RET_HEADS))
    return cos, sin


def _retention(u, b, L, decay_logit, norm_w):
    T = RET_T
    nc = L // T
    dbi, gq, gk, dec = _ret_tables(decay_logit)
    cos, sin = _rope_tables(L)

    def rows(bi, p, c):
        return bi * nc + jnp.where(p == 0, nc - 1 - c, c)

    def rows_p1(bi, p, c):
        return bi * nc + jnp.where(p == 0, 0, c)

    def pos(bi, p, c):
        return jnp.where(p == 0, nc - 1 - c, c)

    const2 = lambda bi, p, c: (0, 0)
    const3 = lambda bi, p, c: (0, 0, 0)
    const4 = lambda bi, p, c: (0, 0, 0, 0)
    return pl.pallas_call(
        functools.partial(_ret_kernel, nc=nc),
        grid=(b, 2, nc),
        in_specs=[pl.BlockSpec((T, 256), lambda bi, p, c: (rows_p1(bi, p, c), U_RQ // 256)),
                  pl.BlockSpec((T, 256), lambda bi, p, c: (rows(bi, p, c), U_RK // 256)),
                  pl.BlockSpec((T, 512), lambda bi, p, c: (rows(bi, p, c), U_RV // 512)),
                  pl.BlockSpec((T, 512), lambda bi, p, c: (rows_p1(bi, p, c), U_RG // 512)),
                  pl.BlockSpec((T, 256), lambda bi, p, c: (pos(bi, p, c), 0)),
                  pl.BlockSpec((T, 256), lambda bi, p, c: (pos(bi, p, c), 0)),
                  pl.BlockSpec((RET_HEADS, T, T), const3),
                  pl.BlockSpec((2, T, 256), const3),
                  pl.BlockSpec((2, T, 256), const3),
                  pl.BlockSpec((2, RET_HEADS // 2, LANES, RET_DV), const4),
                  pl.BlockSpec((1, BRANCH_WIDTH), const2)],
        out_specs=pl.BlockSpec((T, BRANCH_WIDTH), lambda bi, p, c: (rows_p1(bi, p, c), 0)),
        out_shape=jax.ShapeDtypeStruct((b * L, BRANCH_WIDTH), BF16),
        scratch_shapes=[pltpu.VMEM((RET_HEADS // 2, LANES, RET_DV), F32),
                        pltpu.VMEM((RET_HEADS // 2, LANES, RET_DV), F32),
                        pltpu.VMEM((nc, RET_HEADS // 2, LANES, RET_DV), BF16)],
        compiler_params=_cparams(("arbitrary", "arbitrary", "arbitrary")),
        name="retention",
    )(u, u, u, u, cos, sin, dbi, gq, gk, dec, norm_w.reshape(1, BRANCH_WIDTH))


SSD_HALO = 16


def _ssd_kernel(z_ref, x_ref, xp_ref, xn_ref, bc_ref, bcp_ref, bcn_ref, dt_ref,
                cwx_ref, cwbc_ref, cbx_ref, cbbc_ref, dtb_ref, a_ref, dsk_ref, nw_ref,
                tri_ref, trit_ref, ef_ref, eb_ref, o_ref, sf_sc, sb_sc, sbst_sc, *, nc):
    T = SSD_T
    H = SSD_HEADS
    GW = BRANCH_WIDTH // SSD_GROUPS
    p = pl.program_id(1)
    c = pl.program_id(2)
    cc = jnp.where(p == 0, nc - 1 - c, c)
    prev_ok = (cc > 0).astype(F32)
    next_ok = (cc < nc - 1).astype(F32)

    def conv_silu(cur_ref, prev_ref, next_ref, w_ref, b_ref):
        cur = cur_ref[...].astype(F32)
        prev = prev_ref[...].astype(F32) * prev_ok
        nxt = next_ref[...].astype(F32) * next_ok
        row = lax.broadcasted_iota(jnp.int32, cur.shape, 0)
        acc = cur * w_ref[2:3, :] + b_ref[...]
        s = jnp.where(row == 0, prev[SSD_HALO - 1:SSD_HALO], pltpu.roll(cur, 1, 0))
        acc = acc + s * w_ref[1:2, :]
        s = jnp.where(row == 0, prev[SSD_HALO - 2:SSD_HALO - 1],
                      jnp.where(row == 1, prev[SSD_HALO - 1:SSD_HALO], pltpu.roll(cur, 2, 0)))
        acc = acc + s * w_ref[0:1, :]
        s = jnp.where(row == T - 1, nxt[0:1], pltpu.roll(cur, T - 1, 0))
        acc = acc + s * w_ref[3:4, :]
        s = jnp.where(row == T - 2, nxt[0:1],
                      jnp.where(row == T - 1, nxt[1:2], pltpu.roll(cur, T - 2, 0)))
        acc = acc + s * w_ref[4:5, :]
        return _silu(acc)

    def cum(mat, x):
        hi, mid, lo = _split3(x)
        return _dot(mat, hi) + _dot(mat, mid) + _dot(mat, lo)

    def cum_r(x, mat):
        hi, mid, lo = _split3(x)
        return _dot(hi, mat) + _dot(mid, mat) + _dot(lo, mat)

    def expand(x, e_ref):
        hi, lo = _split2(x)
        return _dot(hi, e_ref[...]) + _dot(lo, e_ref[...])

    lane = lax.broadcasted_iota(jnp.int32, (T, LANES), 1)
    is_fwd = lane < H
    xraw = dt_ref[...] + dtb_ref[...]
    dt = jnp.maximum(xraw, 0.0) + jnp.log(1.0 + jnp.exp(-jnp.abs(xraw)))
    dt = jnp.where(lane < 2 * H, dt, 0.0)
    la = dt * a_ref[...]
    tri = tri_ref[...]
    trit = trit_ref[...]
    xs = conv_silu(x_ref, xp_ref, xn_ref, cwx_ref, cbx_ref)
    bcm = conv_silu(bc_ref, bcp_ref, bcn_ref, cwbc_ref, cbbc_ref)
    bm = bcm[:, :LANES].astype(BF16)
    cm = bcm[:, LANES:].astype(BF16)
    lo_grp = lane < SSD_STATE

    def grp_masked(x, g):
        keep = lo_grp if g == 0 else jnp.logical_not(lo_grp)
        return jnp.where(keep, x, jnp.zeros_like(x))

    srow = lax.broadcasted_iota(jnp.int32, (2 * SSD_STATE, BRANCH_WIDTH), 0)
    scol = lax.broadcasted_iota(jnp.int32, (2 * SSD_STATE, BRANCH_WIDTH), 1)
    blockdiag = (srow < SSD_STATE) == (scol < GW)
    rb = cum(trit, la)

    @pl.when(p == 0)
    def _():
        @pl.when(c == 0)
        def _():
            sb_sc[...] = jnp.zeros_like(sb_sc)

        wall = jnp.exp(rb[0:1, :] - rb) * dt
        wb_x = expand(wall, eb_ref)
        dec_x = expand(jnp.exp(rb[0:8, :]), eb_ref)[0:1, :]
        xw = (xs * wb_x).astype(BF16)
        sb = sb_sc[...]
        sbst_sc[cc] = jnp.concatenate([sb[:SSD_STATE, :GW], sb[SSD_STATE:, GW:]], axis=0).astype(BF16)
        sb_sc[...] = sb * dec_x + jnp.where(blockdiag, _dot_tn(bm, xw), 0.0)

    @pl.when(p == 1)
    def _():
        @pl.when(c == 0)
        def _():
            sf_sc[...] = jnp.zeros_like(sf_sc)

        cf = cum(tri, la)
        la_t = la.T[0:2 * H]
        dt_t = dt.T[0:2 * H]
        rf_row = cum_r(la_t, trit)
        rb_row = cum_r(la_t, tri)
        vall = jnp.exp(jnp.where(is_fwd, cf, rb))
        wall = jnp.exp(jnp.where(is_fwd, cf[T - 1:T, :] - cf, 0.0)) * dt
        ef_x = expand(vall, ef_ref)
        eb_x = expand(vall, eb_ref)
        wf_x = expand(wall, ef_ref)
        ii = lax.broadcasted_iota(jnp.int32, (T, T), 0)
        jj = lax.broadcasted_iota(jnp.int32, (T, T), 1)
        lower = jj <= ii
        xsb = xs.astype(BF16)
        y_tiles = []
        for g in range(SSD_GROUPS):
            gmat = _dot_nt(grp_masked(cm, g), bm)
            for pair in range(2):
                tile_idx = 2 * g + pair
                xt = xsb[:, tile_idx * LANES:(tile_idx + 1) * LANES]
                acc = None
                for sub in range(2):
                    h = 2 * tile_idx + sub
                    seg = jnp.where(lower, cf[:, h:h + 1] - rf_row[h:h + 1, :],
                                    rb[:, H + h:H + h + 1] - rb_row[H + h:H + h + 1, :])
                    dsel = jnp.where(jj < ii, dt_t[h:h + 1, :],
                                     jnp.where(jj > ii, dt_t[H + h:H + h + 1, :],
                                               dt_t[h:h + 1, :] + dt_t[H + h:H + h + 1, :]))
                    wm = (gmat * jnp.exp(seg) * dsel).astype(BF16)
                    part = _dot(wm, grp_masked(xt, sub))
                    acc = part if acc is None else acc + part
                y_tiles.append(acc)
        y = jnp.concatenate(y_tiles, axis=1)
        sf = sf_sc[...]
        ycf = _dot(cm, sf.astype(BF16))
        st = sbst_sc[c]
        zero = jnp.zeros((SSD_STATE, GW), BF16)
        sb_next = jnp.concatenate([jnp.concatenate([st[:SSD_STATE], zero], axis=1),
                                   jnp.concatenate([zero, st[SSD_STATE:]], axis=1)], axis=0)
        ycb = _dot(cm, sb_next)
        xwf = (xs * wf_x).astype(BF16)
        sf_sc[...] = sf * ef_x[T - 1:T, :] + jnp.where(blockdiag, _dot_tn(bm, xwf), 0.0)
        y = y + ycf * ef_x + ycb * eb_x + xs * dsk_ref[...]
        z = z_ref[...].astype(F32)
        y = y * _silu(z)
        outs = []
        for g in range(SSD_GROUPS):
            yg = y[:, g * GW:(g + 1) * GW]
            outs.append(yg * lax.rsqrt(jnp.mean(yg * yg, axis=-1, keepdims=True) + RMS_EPS))
        o_ref[...] = (jnp.concatenate(outs, axis=1) * nw_ref[...]).astype(BF16)


def _ssd(u, dt_raw, b, L, conv_w, conv_b, dt_bias, a_log, d_skip, norm_w):
    T = SSD_T
    H = SSD_HEADS
    nc = L // T
    hb = T // SSD_HALO
    n_halo = b * L // SSD_HALO

    def chunk(p, c):
        return jnp.where(p == 0, nc - 1 - c, c)

    def rows(bi, p, c):
        return bi * nc + chunk(p, c)

    def rows_p1(bi, p, c):
        return bi * nc + jnp.where(p == 0, 0, c)

    def prev_rows(bi, p, c):
        return jnp.maximum(rows(bi, p, c) * hb - 1, 0)

    def next_rows(bi, p, c):
        return jnp.minimum((rows(bi, p, c) + 1) * hb, n_halo - 1)

    i = np.arange(T)
    tri = jnp.asarray(i[:, None] >= i[None, :], BF16)
    trit = jnp.asarray(i[:, None] <= i[None, :], BF16)
    hcol = np.arange(BRANCH_WIDTH) // SSD_HEADDIM
    ef = jnp.asarray(np.arange(LANES)[:, None] == hcol[None, :], BF16)
    eb = jnp.asarray(np.arange(LANES)[:, None] == (hcol[None, :] + H), BF16)
    a = -jnp.exp(a_log.astype(F32)).reshape(1, 2 * H)
    a_row = jnp.zeros((1, LANES), F32).at[:, :2 * H].set(a)
    dtb = jnp.zeros((1, LANES), F32).at[:, :2 * H].set(dt_bias.astype(F32).reshape(1, 2 * H))
    dsk = jnp.repeat(d_skip.astype(F32), SSD_HEADDIM).reshape(1, BRANCH_WIDTH)
    cw = conv_w.astype(F32)
    cb = conv_b.astype(F32).reshape(1, -1)
    const2 = lambda bi, p, c: (0, 0)
    return pl.pallas_call(
        functools.partial(_ssd_kernel, nc=nc),
        grid=(b, 2, nc),
        in_specs=[pl.BlockSpec((T, 512), lambda bi, p, c: (rows_p1(bi, p, c), U_SZ // 512)),
                  pl.BlockSpec((T, 512), lambda bi, p, c: (rows(bi, p, c), U_SX // 512)),
                  pl.BlockSpec((SSD_HALO, 512), lambda bi, p, c: (prev_rows(bi, p, c), U_SX // 512)),
                  pl.BlockSpec((SSD_HALO, 512), lambda bi, p, c: (next_rows(bi, p, c), U_SX // 512)),
                  pl.BlockSpec((T, 256), lambda bi, p, c: (rows(bi, p, c), U_SBC // 256)),
                  pl.BlockSpec((SSD_HALO, 256), lambda bi, p, c: (prev_rows(bi, p, c), U_SBC // 256)),
                  pl.BlockSpec((SSD_HALO, 256), lambda bi, p, c: (next_rows(bi, p, c), U_SBC // 256)),
                  pl.BlockSpec((T, LANES), lambda bi, p, c: (rows(bi, p, c), 0)),
                  pl.BlockSpec((SSD_CONV_K, 512), const2),
                  pl.BlockSpec((SSD_CONV_K, 256), const2),
                  pl.BlockSpec((1, 512), const2),
                  pl.BlockSpec((1, 256), const2),
                  pl.BlockSpec((1, LANES), const2),
                  pl.BlockSpec((1, LANES), const2),
                  pl.BlockSpec((1, BRANCH_WIDTH), const2),
                  pl.BlockSpec((1, BRANCH_WIDTH), const2),
                  pl.BlockSpec((T, T), const2),
                  pl.BlockSpec((T, T), const2),
                  pl.BlockSpec((LANES, BRANCH_WIDTH), const2),
                  pl.BlockSpec((LANES, BRANCH_WIDTH), const2)],
        out_specs=pl.BlockSpec((T, BRANCH_WIDTH), lambda bi, p, c: (rows_p1(bi, p, c), 0)),
        out_shape=jax.ShapeDtypeStruct((b * L, BRANCH_WIDTH), BF16),
        scratch_shapes=[pltpu.VMEM((2 * SSD_STATE, BRANCH_WIDTH), F32),
                        pltpu.VMEM((2 * SSD_STATE, BRANCH_WIDTH), F32),
                        pltpu.VMEM((nc, 2 * SSD_STATE, BRANCH_WIDTH // SSD_GROUPS), BF16)],
        compiler_params=_cparams(("arbitrary", "arbitrary", "arbitrary")),
        name="ssd",
    )(u, u, u, u, u, u, u, dt_raw, cw[:, :512], cw[:, 512:], cb[:, :512], cb[:, 512:], dtb, a_row,
      dsk, norm_w.reshape(1, BRANCH_WIDTH), tri, trit, ef, eb)


def _t5_bucket(rel):
    half = NUM_BUCKETS // 2
    max_exact = half // 2
    n = jnp.abs(rel)
    sign = jnp.where(rel > 0, half, 0)
    large = max_exact + (jnp.log(jnp.maximum(n, 1).astype(F32) / max_exact)
                         / math.log(MAX_DISTANCE / max_exact) * (half - max_exact)).astype(jnp.int32)
    large = jnp.minimum(large, half - 1)
    return sign + jnp.where(n < max_exact, n, large)


def _swa_kernel(q_ref, kp_ref, kc_ref, kn_ref, vp_ref, vc_ref, vn_ref, bias_ref, sink_ref, o_ref, *, nb):
    T = SWA_T
    c = pl.program_id(1)
    q = q_ref[...] * jnp.asarray(SWA_HEAD_DIM ** -0.5, BF16)
    k = jnp.concatenate([kp_ref[...], kc_ref[...], kn_ref[...]], axis=0)
    v = jnp.concatenate([vp_ref[...], vc_ref[...], vn_ref[...]], axis=0)
    col = lax.broadcasted_iota(jnp.int32, (T, 3 * T), 1)
    edge = jnp.where(col < T, jnp.where(c > 0, 0.0, NEG_BIG),
                     jnp.where(col >= 2 * T, jnp.where(c < nb - 1, 0.0, NEG_BIG), 0.0))
    lo = lax.broadcasted_iota(jnp.int32, (T, LANES), 1) < SWA_HEAD_DIM
    tiles = []
    for t in range(SWA_HEADS // 2):
        qt = q[:, t * LANES:(t + 1) * LANES]
        halves = []
        for half in range(2):
            h = t + (SWA_HEADS // 2) * half
            keep = lo if half == 0 else jnp.logical_not(lo)
            s = _dot_nt(jnp.where(keep, qt, jnp.zeros_like(qt)), k) + bias_ref[h] + edge
            sk = sink_ref[h]
            m = jnp.maximum(jnp.max(s, axis=-1, keepdims=True), sk)
            pr = jnp.exp(s - m)
            den = jnp.sum(pr, axis=-1, keepdims=True) + jnp.exp(sk - m)
            halves.append(_dot(pr.astype(BF16), v) / den)
        tiles.append(jnp.where(lo, halves[0], halves[1]))
    o_ref[...] = jnp.concatenate(tiles, axis=1).astype(BF16)


def _swa(u, b, L, sink, bias_table):
    T = SWA_T
    nb = L // T
    n_blk = b * nb
    krel = jnp.arange(3 * T) - T
    rel = krel[None, :] - jnp.arange(T)[:, None]
    bias = jnp.transpose(bias_table[_t5_bucket(rel)], (2, 0, 1)).astype(F32)
    bias = jnp.where((jnp.abs(rel) <= WINDOW)[None], bias, NEG_BIG)

    def cur(bi, c):
        return bi * nb + c

    def prv(bi, c):
        return jnp.maximum(bi * nb + c - 1, 0)

    def nxt(bi, c):
        return jnp.minimum(bi * nb + c + 1, n_blk - 1)

    kcol, vcol = U_CK // LANES, U_CV // LANES
    return pl.pallas_call(
        functools.partial(_swa_kernel, nb=nb),
        grid=(b, nb),
        in_specs=[pl.BlockSpec((T, 512), lambda bi, c: (cur(bi, c), U_CQ // 512)),
                  pl.BlockSpec((T, LANES), lambda bi, c: (prv(bi, c), kcol)),
                  pl.BlockSpec((T, LANES), lambda bi, c: (cur(bi, c), kcol)),
                  pl.BlockSpec((T, LANES), lambda bi, c: (nxt(bi, c), kcol)),
                  pl.BlockSpec((T, LANES), lambda bi, c: (prv(bi, c), vcol)),
                  pl.BlockSpec((T, LANES), lambda bi, c: (cur(bi, c), vcol)),
                  pl.BlockSpec((T, LANES), lambda bi, c: (nxt(bi, c), vcol)),
                  pl.BlockSpec((SWA_HEADS, T, 3 * T), lambda bi, c: (0, 0, 0)),
                  pl.BlockSpec(memory_space=pltpu.SMEM)],
        out_specs=pl.BlockSpec((T, BRANCH_WIDTH), lambda bi, c: (cur(bi, c), 0)),
        out_shape=jax.ShapeDtypeStruct((b * L, BRANCH_WIDTH), BF16),
        compiler_params=_cparams(("arbitrary", "arbitrary")),
        name="swa",
    )(u, u, u, u, u, u, u, bias, sink.astype(F32))


def _diff_kernel(q_ref, k_ref, v_ref, bias_ref, cst_ref, lam_ref, nw_ref, o_ref,
                 qs_sc, m_sc, l_sc, acc_sc, *, nq, lam_init):
    T = DIFF_T
    h = pl.program_id(1)
    qi = pl.program_id(2)
    q = q_ref[...] * jnp.asarray(DIFF_DK ** -0.5, BF16)
    lo = lax.broadcasted_iota(jnp.int32, (T, LANES), 1) < DIFF_DK
    zero = jnp.zeros_like(q)
    qs_sc[0:T, :] = jnp.where(lo, q, zero)
    qs_sc[T:2 * T, :] = jnp.where(lo, zero, q)
    m_sc[...] = jnp.full_like(m_sc, NEG_BIG)
    l_sc[...] = jnp.zeros_like(l_sc)
    acc_sc[...] = jnp.zeros_like(acc_sc)

    def tile(kj, bias_tile, cst):
        start = pl.multiple_of(kj * T, T)
        kt = k_ref[pl.ds(start, T), :]
        vt = v_ref[pl.ds(start, T), :]
        s = _dot_nt(qs_sc[...], kt)
        if bias_tile is not None:
            s = (s.reshape(2, T, T) + bias_tile[None]).reshape(2 * T, T)
        m_old = m_sc[...]
        m_new = jnp.maximum(m_old, jnp.max(s, axis=-1, keepdims=True) + cst)
        alpha = jnp.exp(m_old - m_new)
        pr = jnp.exp(s - (m_new - cst))
        l_sc[...] = alpha * l_sc[...] + jnp.sum(pr, axis=-1, keepdims=True)
        acc_sc[...] = alpha * acc_sc[...] + _dot(pr.astype(BF16), vt)
        m_sc[...] = m_new

    c_before = cst_ref[h, 0]
    c_after = cst_ref[h, 1]

    def far_before(kj, carry):
        tile(kj, None, c_before)
        return carry

    def far_after(kj, carry):
        tile(kj, None, c_after)
        return carry

    lax.fori_loop(0, jnp.maximum(qi - 1, 0), far_before, 0)

    @pl.when(qi > 0)
    def _():
        tile(qi - 1, bias_ref[0, 0], 0.0)

    tile(qi, bias_ref[0, 1], 0.0)

    @pl.when(qi < nq - 1)
    def _():
        tile(qi + 1, bias_ref[0, 2], 0.0)

    lax.fori_loop(qi + 2, nq, far_after, 0)

    o = acc_sc[...] / l_sc[...]
    lp = lam_ref[...]
    lam = (jnp.exp(jnp.sum(lp[0:1] * lp[1:2], axis=-1, keepdims=True))
           - jnp.exp(jnp.sum(lp[2:3] * lp[3:4], axis=-1, keepdims=True)) + lam_init)
    out = o[0:T] - lam * o[T:2 * T]
    out = out * lax.rsqrt(jnp.mean(out * out, axis=-1, keepdims=True) + RMS_EPS)
    o_ref[...] = (out * nw_ref[0] * (1.0 - lam_init)).astype(BF16)


def _diff(u, b, L, lam_params, lam_init, norm_w, bias_table):
    T = DIFF_T
    nq = L // T
    d = jnp.arange(T)
    rel = (jnp.arange(3)[:, None, None] - 1) * T + d[None, None, :] - d[None, :, None]
    bias = jnp.transpose(bias_table[_t5_bucket(rel)], (3, 0, 1, 2)).astype(F32)
    far = _t5_bucket(jnp.asarray([-2 * T, 2 * T]))
    cst = bias_table[far].T.astype(F32)
    return pl.pallas_call(
        functools.partial(_diff_kernel, nq=nq, lam_init=lam_init),
        grid=(b, DIFF_HEADS, nq),
        in_specs=[pl.BlockSpec((T, LANES), lambda bi, h, qi: (bi * nq + qi, U_DQ // LANES + h)),
                  pl.BlockSpec((L, LANES), lambda bi, h, qi: (bi, U_DK // LANES + h)),
                  pl.BlockSpec((L, LANES), lambda bi, h, qi: (bi, U_DV // LANES + h)),
                  pl.BlockSpec((1, 3, T, T), lambda bi, h, qi: (h, 0, 0, 0)),
                  pl.BlockSpec(memory_space=pltpu.SMEM),
                  pl.BlockSpec((4, DIFF_DK), lambda bi, h, qi: (0, 0)),
                  pl.BlockSpec((1, 1, DIFF_DV), lambda bi, h, qi: (h, 0, 0))],
        out_specs=pl.BlockSpec((T, DIFF_DV), lambda bi, h, qi: (bi * nq + qi, h)),
        out_shape=jax.ShapeDtypeStruct((b * L, BRANCH_WIDTH), BF16),
        scratch_shapes=[pltpu.VMEM((2 * T, LANES), BF16),
                        pltpu.VMEM((2 * T, 1), F32),
                        pltpu.VMEM((2 * T, 1), F32),
                        pltpu.VMEM((2 * T, DIFF_DV), F32)],
        compiler_params=_cparams(("arbitrary", "arbitrary", "arbitrary")),
        name="diff",
    )(u, u, u, bias, cst, lam_params.astype(F32), norm_w.astype(F32).reshape(DIFF_HEADS, 1, DIFF_DV))


def _const_spec(shape):
    zeros = (0,) * len(shape)
    return pl.BlockSpec(shape, lambda *_: zeros, pipeline_mode=pl.Buffered(1))


def _merge_kernel(x_ref, mod_ref, nw_ref, ya_ref, yb_ref, yc_ref, yd_ref, wg_ref, bg_ref, wb_ref, wo_ref,
                  o_ref):
    x = x_ref[...]
    h = _modulate(x, nw_ref[...], mod_ref[0, 0:1, :], mod_ref[0, 1:2, :]).astype(BF16)
    merged = None
    for n, y_ref in enumerate((ya_ref, yb_ref, yc_ref, yd_ref)):
        cols = slice(n * D_MODEL, (n + 1) * D_MODEL)
        gate = jax.nn.sigmoid(_dot(h, wg_ref[:, cols]) + bg_ref[:, cols])
        term = gate * _dot(y_ref[...], wb_ref[n])
        merged = term if merged is None else merged + term
    o_ref[...] = x + mod_ref[0, 2:3, :] * _dot(merged.astype(BF16), wo_ref[...])


def _merge(x2, mod3, nw, ys, wg, bg, wb, wo, L):
    m = x2.shape[0]
    tm = 512
    per_seq = L // tm
    row = lambda i: (i, 0)
    return pl.pallas_call(
        _merge_kernel,
        grid=(m // tm,),
        in_specs=[pl.BlockSpec((tm, D_MODEL), row),
                  pl.BlockSpec((1, 6, D_MODEL), lambda i: (i // per_seq, 0, 0)),
                  _const_spec((1, D_MODEL))]
                 + [pl.BlockSpec((tm, BRANCH_WIDTH), row)] * 4
                 + [_const_spec((D_MODEL, 4 * D_MODEL)), _const_spec((1, 4 * D_MODEL)),
                    _const_spec((4, BRANCH_WIDTH, D_MODEL)), _const_spec((D_MODEL, D_MODEL))],
        out_specs=pl.BlockSpec((tm, D_MODEL), row),
        out_shape=jax.ShapeDtypeStruct((m, D_MODEL), F32),
        compiler_params=_cparams(("arbitrary",)),
        name="merge",
    )(x2, mod3, nw, *ys, wg, bg, wb, wo)


def _ffn_kernel(x_ref, mod_ref, nw_ref, w1_ref, w3_ref, w2_ref, fw_ref, o_ref, h_sc, acc_sc, *, final):
    j = pl.program_id(1)

    @pl.when(j == 0)
    def _():
        h_sc[...] = _modulate(x_ref[...], nw_ref[...], mod_ref[0, 3:4, :], mod_ref[0, 4:5, :]).astype(BF16)
        acc_sc[...] = jnp.zeros_like(acc_sc)

    h = h_sc[...]
    a = _dot(h, w1_ref[...])
    acc_sc[...] += _dot((_silu(a) * _dot(h, w3_ref[...])).astype(BF16), w2_ref[...])

    @pl.when(j == pl.num_programs(1) - 1)
    def _():
        y = x_ref[...] + mod_ref[0, 5:6, :] * acc_sc[...]
        if final:
            y = y * lax.rsqrt(jnp.mean(y * y, axis=-1, keepdims=True) + RMS_EPS) * fw_ref[...]
        o_ref[...] = y


def _ffn(x2, mod3, nw, w1, w3, w2, fw, L, final):
    m = x2.shape[0]
    tm, th = 512, 1408
    per_seq = L // tm
    return pl.pallas_call(
        functools.partial(_ffn_kernel, final=final),
        grid=(m // tm, FFN_DENSE // th),
        in_specs=[pl.BlockSpec((tm, D_MODEL), lambda i, j: (i, 0)),
                  pl.BlockSpec((1, 6, D_MODEL), lambda i, j: (i // per_seq, 0, 0)),
                  pl.BlockSpec((1, D_MODEL), lambda i, j: (0, 0)),
                  pl.BlockSpec((D_MODEL, th), lambda i, j: (0, j)),
                  pl.BlockSpec((D_MODEL, th), lambda i, j: (0, j)),
                  pl.BlockSpec((th, D_MODEL), lambda i, j: (j, 0)),
                  pl.BlockSpec((1, D_MODEL), lambda i, j: (0, 0))],
        out_specs=pl.BlockSpec((tm, D_MODEL), lambda i, j: (i, 0)),
        out_shape=jax.ShapeDtypeStruct((m, D_MODEL), F32),
        scratch_shapes=[pltpu.VMEM((tm, D_MODEL), BF16), pltpu.VMEM((tm, D_MODEL), F32)],
        compiler_params=_cparams(("arbitrary", "arbitrary")),
        name="ffn",
    )(x2, mod3, nw, w1, w3, w2, fw)


MOE_TM = 512
MOE_BLK = 512
MOE_TH = 512
R_E0, R_E1, R_R0, R_R1, R_G0, R_G1 = range(6)


def _router_kernel(x_ref, mod_ref, nw_ref, wr_ref, tri_ref, h_ref, route_ref, run_sc):
    tm = MOE_TM

    @pl.when(pl.program_id(0) == 0)
    def _():
        run_sc[...] = jnp.zeros_like(run_sc)

    h2 = _modulate(x_ref[...], nw_ref[...], mod_ref[0, 3:4, :], mod_ref[0, 4:5, :])
    hi, lo = _split2(h2)
    h_ref[...] = h2
    logits = _dot(hi, wr_ref[0]) + _dot(lo, wr_ref[0]) + _dot(hi, wr_ref[1])
    lane = lax.broadcasted_iota(jnp.int32, (tm, LANES), 1).astype(F32)
    logits = jnp.where(lane < N_EXPERTS, logits, NEG_BIG)
    m1 = jnp.max(logits, axis=-1, keepdims=True)
    i1 = jnp.min(jnp.where(logits == m1, lane, float(LANES)), axis=-1, keepdims=True)
    oh1 = lane == i1
    rest = jnp.where(oh1, NEG_BIG, logits)
    m2 = jnp.max(rest, axis=-1, keepdims=True)
    i2 = jnp.min(jnp.where(rest == m2, lane, float(LANES)), axis=-1, keepdims=True)
    oh2 = lane == i2
    e = jnp.exp(m2 - m1)
    g1 = 1.0 / (1.0 + e)
    g2 = e * g1
    oh = jnp.where(oh1, 1.0, 0.0) + jnp.where(oh2, 1.0, 0.0)
    before = _dot(tri_ref[...], oh.astype(BF16)) + run_sc[...]
    r1 = jnp.sum(jnp.where(oh1, before, 0.0), axis=-1, keepdims=True)
    r2 = jnp.sum(jnp.where(oh2, before, 0.0), axis=-1, keepdims=True)
    run_sc[...] += jnp.sum(oh, axis=0, keepdims=True)
    rec = jnp.zeros((tm, LANES), F32)
    for pos, val in ((R_E0, i1), (R_E1, i2), (R_R0, r1), (R_R1, r2), (R_G0, g1), (R_G1, g2)):
        rec = jnp.where(lane == float(pos), val, rec)
    route_ref[...] = rec


def _rowcopy_kernel(idx_ref, src_ref, init_ref, out_ref, sem, *, n_step, scatter):
    del init_ref
    base = pl.program_id(0) * n_step

    def issue(i, carry):
        r = base + i
        j = idx_ref[r]
        if scatter:
            cp = pltpu.make_async_copy(src_ref.at[pl.ds(r // TOP_K, 1)], out_ref.at[pl.ds(j, 1)], sem)
        else:
            cp = pltpu.make_async_copy(src_ref.at[pl.ds(j, 1)], out_ref.at[pl.ds(r, 1)], sem)
        cp.start()
        return carry

    lax.fori_loop(0, n_step, issue, 0)

    def drain(i, carry):
        pltpu.make_async_copy(src_ref.at[pl.ds(0, 1)], out_ref.at[pl.ds(0, 1)], sem).wait()
        return carry

    lax.fori_loop(0, n_step, drain, 0)


def _rowcopy(idx, src, n_out, scatter):
    n = idx.shape[0]
    n_step = 1024
    src = src.reshape(src.shape[0], D_MODEL // LANES, LANES)
    init = jnp.zeros((n_out, D_MODEL // LANES, LANES), F32)
    out = pl.pallas_call(
        functools.partial(_rowcopy_kernel, n_step=n_step, scatter=scatter),
        grid_spec=pltpu.PrefetchScalarGridSpec(
            num_scalar_prefetch=1,
            grid=(n // n_step,),
            in_specs=[pl.BlockSpec(memory_space=pl.ANY), pl.BlockSpec(memory_space=pl.ANY)],
            out_specs=pl.BlockSpec(memory_space=pl.ANY),
            scratch_shapes=[pltpu.SemaphoreType.DMA(())]),
        out_shape=jax.ShapeDtypeStruct(init.shape, init.dtype),
        input_output_aliases={2: 0},
        compiler_params=pltpu.CompilerParams(dimension_semantics=("arbitrary",)),
        name="dispatch" if scatter else "collect",
    )(idx, src, init)
    return out.reshape(n_out, D_MODEL)


def _expert_kernel(be_ref, nu_ref, x_ref, w1_ref, w3_ref, w2_ref, o_ref, acc_sc):
    i = pl.program_id(0)
    j = pl.program_id(1)
    last = pl.num_programs(1) - 1

    @pl.when(i < nu_ref[0])
    def _():
        @pl.when(j == 0)
        def _():
            acc_sc[...] = jnp.zeros_like(acc_sc)

        x = x_ref[...].astype(BF16)
        a = _dot(x, w1_ref[0])
        acc_sc[...] += _dot((_silu(a) * _dot(x, w3_ref[0])).astype(BF16), w2_ref[0])

        @pl.when(j == last)
        def _():
            o_ref[...] = acc_sc[...]

    @pl.when(jnp.logical_and(i >= nu_ref[0], j == last))
    def _():
        o_ref[...] = jnp.zeros_like(o_ref)


def _combine_kernel(x_ref, yg_ref, route_ref, mod_ref, fw_ref, o_ref, *, final):
    route = route_ref[...]
    f = (route[:, R_G0:R_G0 + 1] * yg_ref[:, :D_MODEL] + route[:, R_G1:R_G1 + 1] * yg_ref[:, D_MODEL:])
    y = x_ref[...] + mod_ref[0, 5:6, :] * f
    if final:
        y = y * lax.rsqrt(jnp.mean(y * y, axis=-1, keepdims=True) + RMS_EPS) * fw_ref[...]
    o_ref[...] = y


def _moe(x2, mod3, nw, router, w1, w3, w2, fw, L, final):
    t = x2.shape[0]
    tm = MOE_TM
    per_seq = L // tm
    r_hi = jnp.zeros((D_MODEL, LANES), F32).at[:, :N_EXPERTS].set(router.astype(F32))
    r_split = jnp.stack(_split2(r_hi))
    ii = np.arange(tm)
    tri = jnp.asarray(ii[:, None] > ii[None, :], BF16)
    h2, route = pl.pallas_call(
        _router_kernel,
        grid=(t // tm,),
        in_specs=[pl.BlockSpec((tm, D_MODEL), lambda i: (i, 0)),
                  pl.BlockSpec((1, 6, D_MODEL), lambda i: (i // per_seq, 0, 0)),
                  pl.BlockSpec((1, D_MODEL), lambda i: (0, 0)),
                  pl.BlockSpec((2, D_MODEL, LANES), lambda i: (0, 0, 0)),
                  pl.BlockSpec((tm, tm), lambda i: (0, 0))],
        out_specs=[pl.BlockSpec((tm, D_MODEL), lambda i: (i, 0)),
                   pl.BlockSpec((tm, LANES), lambda i: (i, 0))],
        out_shape=[jax.ShapeDtypeStruct((t, D_MODEL), F32),
                   jax.ShapeDtypeStruct((t, LANES), F32)],
        scratch_shapes=[pltpu.VMEM((1, LANES), F32)],
        compiler_params=_cparams(("arbitrary",)),
        name="router",
    )(x2, mod3, nw, r_split, tri)

    n_assign = t * TOP_K
    e = route[:, R_E0:R_E1 + 1].astype(jnp.int32)
    rank = route[:, R_R0:R_R1 + 1].astype(jnp.int32)
    counts = jnp.sum(jax.nn.one_hot(e.reshape(-1), N_EXPERTS, dtype=jnp.int32), axis=0)
    padded = (counts + MOE_BLK - 1) // MOE_BLK * MOE_BLK
    pad_ends = jnp.cumsum(padded)
    pad_starts = pad_ends - padded
    dest = (pad_starts[e] + rank).reshape(-1)
    n_blocks = n_assign // MOE_BLK + N_EXPERTS
    n_rows = n_blocks * MOE_BLK
    block_start = jnp.arange(n_blocks, dtype=jnp.int32) * MOE_BLK
    block_expert = jnp.minimum(jnp.sum(pad_ends[None, :] <= block_start[:, None], axis=1),
                               N_EXPERTS - 1).astype(jnp.int32)
    n_used = (pad_ends[-1:] // MOE_BLK).astype(jnp.int32)

    x_rows = _rowcopy(dest, h2, n_rows, scatter=True)

    nj = FFN_EXPERT // MOE_TH

    def jcol(i, j, be, nu):
        return jnp.where(i < nu[0], j, 0)

    y_rows = pl.pallas_call(
        _expert_kernel,
        grid_spec=pltpu.PrefetchScalarGridSpec(
            num_scalar_prefetch=2,
            grid=(n_blocks, nj),
            in_specs=[pl.BlockSpec((MOE_BLK, D_MODEL), lambda i, j, be, nu: (i, 0)),
                      pl.BlockSpec((1, D_MODEL, MOE_TH), lambda i, j, be, nu: (be[i], 0, jcol(i, j, be, nu))),
                      pl.BlockSpec((1, D_MODEL, MOE_TH), lambda i, j, be, nu: (be[i], 0, jcol(i, j, be, nu))),
                      pl.BlockSpec((1, MOE_TH, D_MODEL), lambda i, j, be, nu: (be[i], jcol(i, j, be, nu), 0))],
            out_specs=pl.BlockSpec((MOE_BLK, D_MODEL), lambda i, j, be, nu: (i, 0)),
            scratch_shapes=[pltpu.VMEM((MOE_BLK, D_MODEL), F32)]),
        out_shape=jax.ShapeDtypeStruct((n_rows, D_MODEL), F32),
        compiler_params=_cparams(("arbitrary", "arbitrary")),
        name="experts",
    )(block_expert, n_used, x_rows, w1, w3, w2)

    yg = _rowcopy(dest, y_rows, n_assign, scatter=False).reshape(t, TOP_K * D_MODEL)

    tc = 512
    return pl.pallas_call(
        functools.partial(_combine_kernel, final=final),
        grid=(t // tc,),
        in_specs=[pl.BlockSpec((tc, D_MODEL), lambda i: (i, 0)),
                  pl.BlockSpec((tc, TOP_K * D_MODEL), lambda i: (i, 0)),
                  pl.BlockSpec((tc, LANES), lambda i: (i, 0)),
                  pl.BlockSpec((1, 6, D_MODEL), lambda i: (i // (L // tc), 0, 0)),
                  pl.BlockSpec((1, D_MODEL), lambda i: (0, 0))],
        out_specs=pl.BlockSpec((tc, D_MODEL), lambda i: (i, 0)),
        out_shape=jax.ShapeDtypeStruct((t, D_MODEL), F32),
        compiler_params=_cparams(("arbitrary",)),
        name="combine",
    )(x2, yg, route, mod3, fw)


_SWA_PERM = ((np.arange(SWA_HEADS // 2)[:, None, None] + (SWA_HEADS // 2) * np.arange(2)[None, :, None])
             * SWA_HEAD_DIM + np.arange(SWA_HEAD_DIM)[None, None, :]).reshape(-1)
_O_SX, _O_SBC, _O_SDT, _O_CQ, _O_CK, _O_CV, _O_DQ = 2048, 2560, 2816, 2832, 3344, 3472, 3600
_U_COLS = np.concatenate([np.arange(0, _O_SX + 512), _O_CQ + _SWA_PERM, np.arange(_O_DQ, _O_DQ + 1536),
                          np.arange(_O_SBC, _O_SBC + 256), np.arange(_O_CK, _O_CK + 128),
                          np.arange(_O_CV, _O_CV + 128)])


def _prep_layer(l, p):
    w_in = p['w_in'][l]
    w = {
        'w_u': w_in[:, _U_COLS].astype(BF16),
        'w_dt': jnp.zeros((D_MODEL, LANES), F32).at[:, :2 * SSD_HEADS].set(
            w_in[:, _O_SDT:_O_SDT + 2 * SSD_HEADS]).astype(BF16),
        'wg': p['w_gate'][l].astype(BF16),
        'bg': p['b_gate'][l].astype(F32).reshape(1, -1),
        'wb': p['w_branch'][l].at[2].set(p['w_branch'][l][2][_SWA_PERM]).astype(BF16),
        'wo': p['w_out'][l].astype(BF16),
    }
    i = l // 2
    if l % 2 == 0:
        w['ffn'] = (p['ffn_w1'][i].astype(BF16), p['ffn_w3'][i].astype(BF16), p['ffn_w2'][i].astype(BF16))
    else:
        w['moe'] = (p['moe_router'][i], p['moe_w1'][i].astype(BF16), p['moe_w3'][i].astype(BF16),
                    p['moe_w2'][i].astype(BF16))
    return w


def _trunk(x, c, p, layers):
    b, L, _ = x.shape
    mod = _ada(c, p['w_ada'], p['b_ada'])
    x2 = x.reshape(b * L, D_MODEL)
    fw = p['final_norm_w'].astype(F32).reshape(1, D_MODEL)
    for l in range(DEPTH):
        w = layers[l]
        final = l == DEPTH - 1
        u, dt = _inproj(x2, mod[l], p['norm_mix_w'][l].reshape(1, D_MODEL), w['w_u'], w['w_dt'], L)
        ya = _retention(u, b, L, p['ret_decay_logit'][l], p['ret_norm_w'][l])
        yb = _ssd(u, dt, b, L, p['ssd_conv_w'][l], p['ssd_conv_b'][l], p['ssd_dt_bias'][l],
                  p['ssd_a_log'][l], p['ssd_d'][l], p['ssd_norm_w'][l])
        yc = _swa(u, b, L, p['swa_sink'][l], p['rel_bias'][:, :SWA_HEADS])
        yd = _diff(u, b, L, p['diff_lambda'][l], 0.8 - 0.6 * math.exp(-0.3 * l), p['diff_norm_w'][l],
                   p['rel_bias'][:, SWA_HEADS:])
        x2 = _merge(x2, mod[l], p['norm_mix_w'][l].reshape(1, D_MODEL), (ya, yb, yc, yd),
                    w['wg'], w['bg'], w['wb'], w['wo'], L)
        nw = p['norm_ffn_w'][l].reshape(1, D_MODEL)
        if l % 2 == 0:
            x2 = _ffn(x2, mod[l], nw, *w['ffn'], fw, L, final)
        else:
            x2 = _moe(x2, mod[l], nw, *w['moe'], fw, L, final)
    return x2.reshape(b, L, D_MODEL)


def kernel(x_prompt, x_sample, c_prompt, c_sample, norm_mix_w, norm_ffn_w, w_ada, b_ada, w_in,
           ret_decay_logit, ret_norm_w, ssd_conv_w, ssd_conv_b, ssd_dt_bias, ssd_a_log, ssd_d,
           ssd_norm_w, swa_sink, diff_lambda, diff_norm_w, rel_bias, w_gate, b_gate, w_branch, w_out,
           ffn_w1, ffn_w3, ffn_w2, moe_router, moe_w1, moe_w3, moe_w2, final_norm_w):
    p = {'norm_mix_w': norm_mix_w, 'norm_ffn_w': norm_ffn_w, 'w_ada': w_ada, 'b_ada': b_ada,
         'w_in': w_in, 'ret_decay_logit': ret_decay_logit, 'ret_norm_w': ret_norm_w,
         'ssd_conv_w': ssd_conv_w, 'ssd_conv_b': ssd_conv_b, 'ssd_dt_bias': ssd_dt_bias,
         'ssd_a_log': ssd_a_log, 'ssd_d': ssd_d, 'ssd_norm_w': ssd_norm_w, 'swa_sink': swa_sink,
         'diff_lambda': diff_lambda, 'diff_norm_w': diff_norm_w, 'rel_bias': rel_bias,
         'w_gate': w_gate, 'b_gate': b_gate, 'w_branch': w_branch, 'w_out': w_out,
         'ffn_w1': ffn_w1, 'ffn_w3': ffn_w3, 'ffn_w2': ffn_w2, 'moe_router': moe_router,
         'moe_w1': moe_w1, 'moe_w3': moe_w3, 'moe_w2': moe_w2, 'final_norm_w': final_norm_w}
    layers = [_prep_layer(l, p) for l in range(DEPTH)]
    return (_trunk(x_prompt, c_prompt, p, layers), _trunk(x_sample, c_sample, p, layers))
```

```python
import functools
import math

import numpy as np
import jax
import jax.numpy as jnp
from jax import lax
from jax.experimental import pallas as pl
from jax.experimental.pallas import tpu as pltpu

F32 = jnp.float32
BF16 = jnp.bfloat16

D_MODEL = 1024
DEPTH = 2
BRANCH_WIDTH = 512
RMS_EPS = 1e-6
RET_HEADS, RET_DK, RET_DV = 4, 64, 128
ROPE_BASE = 10000.0
SSD_HEADS, SSD_HEADDIM, SSD_GROUPS, SSD_STATE, SSD_CONV_K = 8, 64, 2, 64, 5
SWA_HEADS, SWA_KV_HEADS, SWA_HEAD_DIM, WINDOW = 8, 2, 64, 128
DIFF_HEADS, DIFF_DK, DIFF_DV = 4, 64, 128
NUM_BUCKETS, MAX_DISTANCE = 32, 128
FFN_DENSE = 2816
N_EXPERTS, TOP_K, FFN_EXPERT = 8, 2, 3584
LANES = 128
NEG_BIG = -1e30

U_RQ, U_RK, U_RV, U_RG = 0, 256, 512, 1024
U_SZ, U_SX = 1536, 2048
U_CQ, U_DQ, U_DK, U_DV = 2560, 3072, 3584, 4096
U_SBC, U_CK, U_CV = 4608, 4864, 4992
U_WIDTH = 5120

RET_T = 256
SSD_T = 128
SWA_T = 128
DIFF_T = 512
VMEM_LIMIT = 56 * 1024 * 1024


def _cparams(sem):
    return pltpu.CompilerParams(dimension_semantics=sem, vmem_limit_bytes=VMEM_LIMIT)


def _dot(a, b):
    return jnp.dot(a, b, preferred_element_type=F32)


def _dot_nt(a, b):
    return lax.dot_general(a, b, (((1,), (1,)), ((), ())), preferred_element_type=F32)


def _dot_tn(a, b):
    return lax.dot_general(a, b, (((0,), (0,)), ((), ())), preferred_element_type=F32)


def _silu(x):
    return x * jax.nn.sigmoid(x)


def _split2(x):
    hi = x.astype(BF16)
    lo = (x - hi.astype(F32)).astype(BF16)
    return hi, lo


def _split3(x):
    hi = x.astype(BF16)
    r = x - hi.astype(F32)
    mid = r.astype(BF16)
    lo = (r - mid.astype(F32)).astype(BF16)
    return hi, mid, lo


def _modulate(x, nw, shift, scale):
    y = x * lax.rsqrt(jnp.mean(x * x, axis=-1, keepdims=True) + RMS_EPS) * nw
    return y * (1.0 + scale) + shift


def _ada_kernel(c_ref, w_ref, b_ref, o_ref):
    s = _silu(c_ref[...]).astype(BF16)
    o_ref[0] = _dot(s, w_ref[0].astype(BF16)) + b_ref[0]


def _ada(c, w_ada, b_ada):
    b = c.shape[0]
    cp = jnp.zeros((8, D_MODEL), F32).at[:b].set(c)
    tn = 1536
    out = pl.pallas_call(
        _ada_kernel,
        grid=(DEPTH, 6 * D_MODEL // tn),
        in_specs=[pl.BlockSpec((8, D_MODEL), lambda l, j: (0, 0)),
                  pl.BlockSpec((1, D_MODEL, tn), lambda l, j: (l, 0, j)),
                  pl.BlockSpec((1, 1, tn), lambda l, j: (l, 0, j))],
        out_specs=pl.BlockSpec((1, 8, tn), lambda l, j: (l, 0, j)),
        out_shape=jax.ShapeDtypeStruct((DEPTH, 8, 6 * D_MODEL), F32),
        compiler_params=_cparams(("arbitrary", "arbitrary")),
        name="ada",
    )(cp, w_ada, b_ada.reshape(DEPTH, 1, 6 * D_MODEL))
    return out[:, :b].reshape(DEPTH, b, 6, D_MODEL)


def _inproj_kernel(x_ref, mod_ref, nw_ref, w_ref, wdt_ref, u_ref, dt_ref, h_sc):
    @pl.when(pl.program_id(1) == 0)
    def _():
        h = _modulate(x_ref[...], nw_ref[...], mod_ref[0, 0:1, :], mod_ref[0, 1:2, :]).astype(BF16)
        h_sc[...] = h
        dt_ref[...] = _dot(h, wdt_ref[...])

    u_ref[...] = _dot(h_sc[...], w_ref[...]).astype(BF16)


def _inproj(x2, mod3, nw, w_u, w_dt, L):
    m = x2.shape[0]
    tm, tn = 1024, 512
    per_seq = L // tm
    return pl.pallas_call(
        _inproj_kernel,
        grid=(m // tm, U_WIDTH // tn),
        in_specs=[pl.BlockSpec((tm, D_MODEL), lambda i, j: (i, 0)),
                  pl.BlockSpec((1, 6, D_MODEL), lambda i, j: (i // per_seq, 0, 0)),
                  pl.BlockSpec((1, D_MODEL), lambda i, j: (0, 0)),
                  pl.BlockSpec((D_MODEL, tn), lambda i, j: (0, j)),
                  pl.BlockSpec((D_MODEL, LANES), lambda i, j: (0, 0))],
        out_specs=[pl.BlockSpec((tm, tn), lambda i, j: (i, j)),
                   pl.BlockSpec((tm, LANES), lambda i, j: (i, 0))],
        out_shape=[jax.ShapeDtypeStruct((m, U_WIDTH), BF16),
                   jax.ShapeDtypeStruct((m, LANES), F32)],
        scratch_shapes=[pltpu.VMEM((tm, D_MODEL), BF16)],
        compiler_params=_cparams(("arbitrary", "arbitrary")),
        name="inproj",
    )(x2, mod3, nw, w_u, w_dt)


def _ret_kernel(q_ref, k_ref, v_ref, g_ref, cos_ref, sin_ref, dbi_ref, gq_ref, gk_ref, dec_ref,
                nw_ref, o_ref, sf_sc, sb_sc, sbst_sc, *, nc):
    T = RET_T
    p = pl.program_id(1)
    c = pl.program_id(2)
    cos = cos_ref[...]
    sin = sin_ref[...]
    lane = lax.broadcasted_iota(jnp.int32, (T, 2 * LANES), 1)
    first_half = (lane % RET_DK) < (RET_DK // 2)
    lo_head = lax.broadcasted_iota(jnp.int32, (T, LANES), 1) < RET_DK

    def rope(ref):
        x = ref[...].astype(F32)
        rot = jnp.where(first_half, pltpu.roll(x, 2 * LANES - RET_DK // 2, 1),
                        pltpu.roll(x, RET_DK // 2, 1))
        return x * cos + rot * sin

    def head_masked(x, h):
        t = x[:, (h // 2) * LANES:(h // 2 + 1) * LANES]
        keep = lo_head if h % 2 == 0 else jnp.logical_not(lo_head)
        return jnp.where(keep, t, jnp.zeros_like(t))

    @pl.when(p == 0)
    def _():
        @pl.when(c == 0)
        def _():
            sb_sc[...] = jnp.zeros_like(sb_sc)

        kb = (rope(k_ref) * (RET_DK ** -0.5) * gk_ref[1]).astype(BF16)
        v = v_ref[...]
        cc = nc - 1 - c
        for t in range(RET_HEADS // 2):
            sb = sb_sc[t]
            sbst_sc[cc, t] = sb.astype(BF16)
            upd = (_dot_tn(head_masked(kb, 2 * t), v[:, (2 * t) * RET_DV:(2 * t + 1) * RET_DV])
                   + _dot_tn(head_masked(kb, 2 * t + 1), v[:, (2 * t + 1) * RET_DV:(2 * t + 2) * RET_DV]))
            sb_sc[t] = sb * dec_ref[1, t] + upd

    @pl.when(p == 1)
    def _():
        @pl.when(c == 0)
        def _():
            sf_sc[...] = jnp.zeros_like(sf_sc)

        qr = rope(q_ref)
        kr = rope(k_ref) * (RET_DK ** -0.5)
        qb = qr.astype(BF16)
        kb = kr.astype(BF16)
        qf = (qr * gq_ref[0]).astype(BF16)
        qw = (qr * gq_ref[1]).astype(BF16)
        kf = (kr * gk_ref[0]).astype(BF16)
        v = v_ref[...]
        ys = []
        upds = []
        for h in range(RET_HEADS):
            t = h // 2
            vh = v[:, h * RET_DV:(h + 1) * RET_DV]
            kt = kb[:, t * LANES:(t + 1) * LANES]
            s = _dot_nt(head_masked(qb, h), kt)
            y = _dot((s * dbi_ref[h]).astype(BF16), vh)
            y = y + _dot(head_masked(qf, h), sf_sc[t].astype(BF16))
            y = y + _dot(head_masked(qw, h), sbst_sc[c, t])
            upds.append(_dot_tn(head_masked(kf, h), vh))
            mu = jnp.mean(y, axis=-1, keepdims=True)
            yc = y - mu
            var = jnp.mean(yc * yc, axis=-1, keepdims=True)
            ys.append(yc * lax.rsqrt(var + RMS_EPS))
        for t in range(RET_HEADS // 2):
            sf_sc[t] = sf_sc[t] * dec_ref[0, t] + upds[2 * t] + upds[2 * t + 1]
        y = jnp.concatenate(ys, axis=1) * nw_ref[...]
        g = g_ref[...].astype(F32)
        o_ref[...] = (_silu(g) * y).astype(BF16)


def _ret_tables(decay_logit):
    T = RET_T
    lg = jax.nn.log_sigmoid(decay_logit.astype(F32))
    i = jnp.arange(T, dtype=F32)
    diff = i[:, None] - i[None, :]
    dbi = jnp.where(diff[None] >= 0, jnp.exp(lg[0][:, None, None] * diff[None]),
                    jnp.exp(-lg[1][:, None, None] * diff[None]))

    def cols(per_head):
        return jnp.repeat(per_head.T, RET_DK, axis=1)

    gq = jnp.stack([cols(jnp.exp(lg[0][:, None] * (i + 1)[None])),
                    cols(jnp.exp(lg[1][:, None] * (T - i)[None]))])
    gk = jnp.stack([cols(jnp.exp(lg[0][:, None] * (T - 1 - i)[None])),
                    cols(jnp.exp(lg[1][:, None] * i[None]))])
    dec = jnp.exp(lg * T)
    dec = jnp.repeat(dec.reshape(2, RET_HEADS // 2, 2, 1), RET_DK, axis=2).reshape(2, RET_HEADS // 2, LANES, 1)
    dec = jnp.broadcast_to(dec, (2, RET_HEADS // 2, LANES, RET_DV))
    return dbi, gq, gk, dec


def _rope_tables(L):
    half = RET_DK // 2
    inv = ROPE_BASE ** (-jnp.arange(half, dtype=F32) / half)
    ang = jnp.arange(L, dtype=F32)[:, None] * inv[None, :]
    cos = jnp.tile(jnp.concatenate([jnp.cos(ang), jnp.cos(ang)], axis=1), (1, RET_HEADS))
    sin = jnp.tile(jnp.concatenate([-jnp.sin(ang), jnp.sin(ang)], axis=1), (1, RET_HEADS))
    return cos, sin


def _retention(u, b, L, decay_logit, norm_w):
    T = RET_T
    nc = L // T
    dbi, gq, gk, dec = _ret_tables(decay_logit)
    cos, sin = _rope_tables(L)

    def rows(bi, p, c):
        return bi * nc + jnp.where(p == 0, nc - 1 - c, c)

    def rows_p1(bi, p, c):
        return bi * nc + jnp.where(p == 0, 0, c)

    def pos(bi, p, c):
        return jnp.where(p == 0, nc - 1 - c, c)

    const2 = lambda bi, p, c: (0, 0)
    const3 = lambda bi, p, c: (0, 0, 0)
    const4 = lambda bi, p, c: (0, 0, 0, 0)
    return pl.pallas_call(
        functools.partial(_ret_kernel, nc=nc),
        grid=(b, 2, nc),
        in_specs=[pl.BlockSpec((T, 256), lambda bi, p, c: (rows_p1(bi, p, c), U_RQ // 256)),
                  pl.BlockSpec((T, 256), lambda bi, p, c: (rows(bi, p, c), U_RK // 256)),
                  pl.BlockSpec((T, 512), lambda bi, p, c: (rows(bi, p, c), U_RV // 512)),
                  pl.BlockSpec((T, 512), lambda bi, p, c: (rows_p1(bi, p, c), U_RG // 512)),
                  pl.BlockSpec((T, 256), lambda bi, p, c: (pos(bi, p, c), 0)),
                  pl.BlockSpec((T, 256), lambda bi, p, c: (pos(bi, p, c), 0)),
                  pl.BlockSpec((RET_HEADS, T, T), const3),
                  pl.BlockSpec((2, T, 256), const3),
                  pl.BlockSpec((2, T, 256), const3),
                  pl.BlockSpec((2, RET_HEADS // 2, LANES, RET_DV), const4),
                  pl.BlockSpec((1, BRANCH_WIDTH), const2)],
        out_specs=pl.BlockSpec((T, BRANCH_WIDTH), lambda bi, p, c: (rows_p1(bi, p, c), 0)),
        out_shape=jax.ShapeDtypeStruct((b * L, BRANCH_WIDTH), BF16),
        scratch_shapes=[pltpu.VMEM((RET_HEADS // 2, LANES, RET_DV), F32),
                        pltpu.VMEM((RET_HEADS // 2, LANES, RET_DV), F32),
                        pltpu.VMEM((nc, RET_HEADS // 2, LANES, RET_DV), BF16)],
        compiler_params=_cparams(("arbitrary", "arbitrary", "arbitrary")),
        name="retention",
    )(u, u, u, u, cos, sin, dbi, gq, gk, dec, norm_w.reshape(1, BRANCH_WIDTH))


SSD_HALO = 16


def _ssd_kernel(z_ref, x_ref, xp_ref, xn_ref, bc_ref, bcp_ref, bcn_ref, dt_ref,
                cwx_ref, cwbc_ref, cbx_ref, cbbc_ref, dtb_ref, a_ref, dsk_ref, nw_ref,
                tri_ref, trit_ref, ef_ref, eb_ref, o_ref, sf_sc, sb_sc, sbst_sc, *, nc):
    T = SSD_T
    H = SSD_HEADS
    GW = BRANCH_WIDTH // SSD_GROUPS
    p = pl.program_id(1)
    c = pl.program_id(2)
    cc = jnp.where(p == 0, nc - 1 - c, c)
    prev_ok = (cc > 0).astype(F32)
    next_ok = (cc < nc - 1).astype(F32)

    def conv_silu(cur_ref, prev_ref, next_ref, w_ref, b_ref):
        cur = cur_ref[...].astype(F32)
        prev = prev_ref[...].astype(F32) * prev_ok
        nxt = next_ref[...].astype(F32) * next_ok
        row = lax.broadcasted_iota(jnp.int32, cur.shape, 0)
        acc = cur * w_ref[2:3, :] + b_ref[...]
        s = jnp.where(row == 0, prev[SSD_HALO - 1:SSD_HALO], pltpu.roll(cur, 1, 0))
        acc = acc + s * w_ref[1:2, :]
        s = jnp.where(row == 0, prev[SSD_HALO - 2:SSD_HALO - 1],
                      jnp.where(row == 1, prev[SSD_HALO - 1:SSD_HALO], pltpu.roll(cur, 2, 0)))
        acc = acc + s * w_ref[0:1, :]
        s = jnp.where(row == T - 1, nxt[0:1], pltpu.roll(cur, T - 1, 0))
        acc = acc + s * w_ref[3:4, :]
        s = jnp.where(row == T - 2, nxt[0:1],
                      jnp.where(row == T - 1, nxt[1:2], pltpu.roll(cur, T - 2, 0)))
        acc = acc + s * w_ref[4:5, :]
        return _silu(acc)

    def cum(mat, x):
        hi, mid, lo = _split3(x)
        return _dot(mat, hi) + _dot(mat, mid) + _dot(mat, lo)

    def cum_r(x, mat):
        hi, mid, lo = _split3(x)
        return _dot(hi, mat) + _dot(mid, mat) + _dot(lo, mat)

    def expand(x, e_ref):
        hi, lo = _split2(x)
        return _dot(hi, e_ref[...]) + _dot(lo, e_ref[...])

    lane = lax.broadcasted_iota(jnp.int32, (T, LANES), 1)
    is_fwd = lane < H
    xraw = dt_ref[...] + dtb_ref[...]
    dt = jnp.maximum(xraw, 0.0) + jnp.log(1.0 + jnp.exp(-jnp.abs(xraw)))
    dt = jnp.where(lane < 2 * H, dt, 0.0)
    la = dt * a_ref[...]
    tri = tri_ref[...]
    trit = trit_ref[...]
    xs = conv_silu(x_ref, xp_ref, xn_ref, cwx_ref, cbx_ref)
    bcm = conv_silu(bc_ref, bcp_ref, bcn_ref, cwbc_ref, cbbc_ref)
    bm = bcm[:, :LANES].astype(BF16)
    cm = bcm[:, LANES:].astype(BF16)
    lo_grp = lane < SSD_STATE

    def grp_masked(x, g):
        keep = lo_grp if g == 0 else jnp.logical_not(lo_grp)
        return jnp.where(keep, x, jnp.zeros_like(x))

    srow = lax.broadcasted_iota(jnp.int32, (2 * SSD_STATE, BRANCH_WIDTH), 0)
    scol = lax.broadcasted_iota(jnp.int32, (2 * SSD_STATE, BRANCH_WIDTH), 1)
    blockdiag = (srow < SSD_STATE) == (scol < GW)
    rb = cum(trit, la)

    @pl.when(p == 0)
    def _():
        @pl.when(c == 0)
        def _():
            sb_sc[...] = jnp.zeros_like(sb_sc)

        wall = jnp.exp(rb[0:1, :] - rb) * dt
        wb_x = expand(wall, eb_ref)
        dec_x = expand(jnp.exp(rb[0:8, :]), eb_ref)[0:1, :]
        xw = (xs * wb_x).astype(BF16)
        sb = sb_sc[...]
        sbst_sc[cc] = jnp.concatenate([sb[:SSD_STATE, :GW], sb[SSD_STATE:, GW:]], axis=0).astype(BF16)
        sb_sc[...] = sb * dec_x + jnp.where(blockdiag, _dot_tn(bm, xw), 0.0)

    @pl.when(p == 1)
    def _():
        @pl.when(c == 0)
        def _():
            sf_sc[...] = jnp.zeros_like(sf_sc)

        cf = cum(tri, la)
        la_t = la.T[0:2 * H]
        dt_t = dt.T[0:2 * H]
        rf_row = cum_r(la_t, trit)
        rb_row = cum_r(la_t, tri)
        vall = jnp.exp(jnp.where(is_fwd, cf, rb))
        wall = jnp.exp(jnp.where(is_fwd, cf[T - 1:T, :] - cf, 0.0)) * dt
        ef_x = expand(vall, ef_ref)
        eb_x = expand(vall, eb_ref)
        wf_x = expand(wall, ef_ref)
        ii = lax.broadcasted_iota(jnp.int32, (T, T), 0)
        jj = lax.broadcasted_iota(jnp.int32, (T, T), 1)
        lower = jj <= ii
        xsb = xs.astype(BF16)
        y_tiles = []
        for g in range(SSD_GROUPS):
            gmat = _dot_nt(grp_masked(cm, g), bm)
            for pair in range(2):
                tile_idx = 2 * g + pair
                xt = xsb[:, tile_idx * LANES:(tile_idx + 1) * LANES]
                acc = None
                for sub in range(2):
                    h = 2 * tile_idx + sub
                    seg = jnp.where(lower, cf[:, h:h + 1] - rf_row[h:h + 1, :],
                                    rb[:, H + h:H + h + 1] - rb_row[H + h:H + h + 1, :])
                    dsel = jnp.where(jj < ii, dt_t[h:h + 1, :],
                                     jnp.where(jj > ii, dt_t[H + h:H + h + 1, :],
                                               dt_t[h:h + 1, :] + dt_t[H + h:H + h + 1, :]))
                    wm = (gmat * jnp.exp(seg) * dsel).astype(BF16)
                    part = _dot(wm, grp_masked(xt, sub))
                    acc = part if acc is None else acc + part
                y_tiles.append(acc)
        y = jnp.concatenate(y_tiles, axis=1)
        sf = sf_sc[...]
        ycf = _dot(cm, sf.astype(BF16))
        st = sbst_sc[c]
        zero = jnp.zeros((SSD_STATE, GW), BF16)
        sb_next = jnp.concatenate([jnp.concatenate([st[:SSD_STATE], zero], axis=1),
                                   jnp.concatenate([zero, st[SSD_STATE:]], axis=1)], axis=0)
        ycb = _dot(cm, sb_next)
        xwf = (xs * wf_x).astype(BF16)
        sf_sc[...] = sf * ef_x[T - 1:T, :] + jnp.where(blockdiag, _dot_tn(bm, xwf), 0.0)
        y = y + ycf * ef_x + ycb * eb_x + xs * dsk_ref[...]
        z = z_ref[...].astype(F32)
        y = y * _silu(z)
        outs = []
        for g in range(SSD_GROUPS):
            yg = y[:, g * GW:(g + 1) * GW]
            outs.append(yg * lax.rsqrt(jnp.mean(yg * yg, axis=-1, keepdims=True) + RMS_EPS))
        o_ref[...] = (jnp.concatenate(outs, axis=1) * nw_ref[...]).astype(BF16)


def _ssd(u, dt_raw, b, L, conv_w, conv_b, dt_bias, a_log, d_skip, norm_w):
    T = SSD_T
    H = SSD_HEADS
    nc = L // T
    hb = T // SSD_HALO
    n_halo = b * L // SSD_HALO

    def chunk(p, c):
        return jnp.where(p == 0, nc - 1 - c, c)

    def rows(bi, p, c):
        return bi * nc + chunk(p, c)

    def rows_p1(bi, p, c):
        return bi * nc + jnp.where(p == 0, 0, c)

    def prev_rows(bi, p, c):
        return jnp.maximum(rows(bi, p, c) * hb - 1, 0)

    def next_rows(bi, p, c):
        return jnp.minimum((rows(bi, p, c) + 1) * hb, n_halo - 1)

    i = np.arange(T)
    tri = jnp.asarray(i[:, None] >= i[None, :], BF16)
    trit = jnp.asarray(i[:, None] <= i[None, :], BF16)
    hcol = np.arange(BRANCH_WIDTH) // SSD_HEADDIM
    ef = jnp.asarray(np.arange(LANES)[:, None] == hcol[None, :], BF16)
    eb = jnp.asarray(np.arange(LANES)[:, None] == (hcol[None, :] + H), BF16)
    a = -jnp.exp(a_log.astype(F32)).reshape(1, 2 * H)
    a_row = jnp.zeros((1, LANES), F32).at[:, :2 * H].set(a)
    dtb = jnp.zeros((1, LANES), F32).at[:, :2 * H].set(dt_bias.astype(F32).reshape(1, 2 * H))
    dsk = jnp.repeat(d_skip.astype(F32), SSD_HEADDIM).reshape(1, BRANCH_WIDTH)
    cw = conv_w.astype(F32)
    cb = conv_b.astype(F32).reshape(1, -1)
    const2 = lambda bi, p, c: (0, 0)
    return pl.pallas_call(
        functools.partial(_ssd_kernel, nc=nc),
        grid=(b, 2, nc),
        in_specs=[pl.BlockSpec((T, 512), lambda bi, p, c: (rows_p1(bi, p, c), U_SZ // 512)),
                  pl.BlockSpec((T, 512), lambda bi, p, c: (rows(bi, p, c), U_SX // 512)),
                  pl.BlockSpec((SSD_HALO, 512), lambda bi, p, c: (prev_rows(bi, p, c), U_SX // 512)),
                  pl.BlockSpec((SSD_HALO, 512), lambda bi, p, c: (next_rows(bi, p, c), U_SX // 512)),
                  pl.BlockSpec((T, 256), lambda bi, p, c: (rows(bi, p, c), U_SBC // 256)),
                  pl.BlockSpec((SSD_HALO, 256), lambda bi, p, c: (prev_rows(bi, p, c), U_SBC // 256)),
                  pl.BlockSpec((SSD_HALO, 256), lambda bi, p, c: (next_rows(bi, p, c), U_SBC // 256)),
                  pl.BlockSpec((T, LANES), lambda bi, p, c: (rows(bi, p, c), 0)),
                  pl.BlockSpec((SSD_CONV_K, 512), const2),
                  pl.BlockSpec((SSD_CONV_K, 256), const2),
                  pl.BlockSpec((1, 512), const2),
                  pl.BlockSpec((1, 256), const2),
                  pl.BlockSpec((1, LANES), const2),
                  pl.BlockSpec((1, LANES), const2),
                  pl.BlockSpec((1, BRANCH_WIDTH), const2),
                  pl.BlockSpec((1, BRANCH_WIDTH), const2),
                  pl.BlockSpec((T, T), const2),
                  pl.BlockSpec((T, T), const2),
                  pl.BlockSpec((LANES, BRANCH_WIDTH), const2),
                  pl.BlockSpec((LANES, BRANCH_WIDTH), const2)],
        out_specs=pl.BlockSpec((T, BRANCH_WIDTH), lambda bi, p, c: (rows_p1(bi, p, c), 0)),
        out_shape=jax.ShapeDtypeStruct((b * L, BRANCH_WIDTH), BF16),
        scratch_shapes=[pltpu.VMEM((2 * SSD_STATE, BRANCH_WIDTH), F32),
                        pltpu.VMEM((2 * SSD_STATE, BRANCH_WIDTH), F32),
                        pltpu.VMEM((nc, 2 * SSD_STATE, BRANCH_WIDTH // SSD_GROUPS), BF16)],
        compiler_params=_cparams(("arbitrary", "arbitrary", "arbitrary")),
        name="ssd",
    )(u, u, u, u, u, u, u, dt_raw, cw[:, :512], cw[:, 512:], cb[:, :512], cb[:, 512:], dtb, a_row,
      dsk, norm_w.reshape(1, BRANCH_WIDTH), tri, trit, ef, eb)


def _t5_bucket(rel):
    half = NUM_BUCKETS // 2
    max_exact = half // 2
    n = jnp.abs(rel)
    sign = jnp.where(rel > 0, half, 0)
    large = max_exact + (jnp.log(jnp.maximum(n, 1).astype(F32) / max_exact)
                         / math.log(MAX_DISTANCE / max_exact) * (half - max_exact)).astype(jnp.int32)
    large = jnp.minimum(large, half - 1)
    return sign + jnp.where(n < max_exact, n, large)


def _swa_kernel(q_ref, kp_ref, kc_ref, kn_ref, vp_ref, vc_ref, vn_ref, bias_ref, sink_ref, o_ref, *, nb):
    T = SWA_T
    c = pl.program_id(1)
    q = q_ref[...] * jnp.asarray(SWA_HEAD_DIM ** -0.5, BF16)
    k = jnp.concatenate([kp_ref[...], kc_ref[...], kn_ref[...]], axis=0)
    v = jnp.concatenate([vp_ref[...], vc_ref[...], vn_ref[...]], axis=0)
    col = lax.broadcasted_iota(jnp.int32, (T, 3 * T), 1)
    edge = jnp.where(col < T, jnp.where(c > 0, 0.0, NEG_BIG),
                     jnp.where(col >= 2 * T, jnp.where(c < nb - 1, 0.0, NEG_BIG), 0.0))
    lo = lax.broadcasted_iota(jnp.int32, (T, LANES), 1) < SWA_HEAD_DIM
    tiles = []
    for t in range(SWA_HEADS // 2):
        qt = q[:, t * LANES:(t + 1) * LANES]
        halves = []
        for half in range(2):
            h = t + (SWA_HEADS // 2) * half
            keep = lo if half == 0 else jnp.logical_not(lo)
            s = _dot_nt(jnp.where(keep, qt, jnp.zeros_like(qt)), k) + bias_ref[h] + edge
            sk = sink_ref[h]
            m = jnp.maximum(jnp.max(s, axis=-1, keepdims=True), sk)
            pr = jnp.exp(s - m)
            den = jnp.sum(pr, axis=-1, keepdims=True) + jnp.exp(sk - m)
            halves.append(_dot(pr.astype(BF16), v) / den)
        tiles.append(jnp.where(lo, halves[0], halves[1]))
    o_ref[...] = jnp.concatenate(tiles, axis=1).astype(BF16)


def _swa(u, b, L, sink, bias_table):
    T = SWA_T
    nb = L // T
    n_blk = b * nb
    krel = jnp.arange(3 * T) - T
    rel = krel[None, :] - jnp.arange(T)[:, None]
    bias = jnp.transpose(bias_table[_t5_bucket(rel)], (2, 0, 1)).astype(F32)
    bias = jnp.where((jnp.abs(rel) <= WINDOW)[None], bias, NEG_BIG)

    def cur(bi, c):
        return bi * nb + c

    def prv(bi, c):
        return jnp.maximum(bi * nb + c - 1, 0)

    def nxt(bi, c):
        return jnp.minimum(bi * nb + c + 1, n_blk - 1)

    kcol, vcol = U_CK // LANES, U_CV // LANES
    return pl.pallas_call(
        functools.partial(_swa_kernel, nb=nb),
        grid=(b, nb),
        in_specs=[pl.BlockSpec((T, 512), lambda bi, c: (cur(bi, c), U_CQ // 512)),
                  pl.BlockSpec((T, LANES), lambda bi, c: (prv(bi, c), kcol)),
                  pl.BlockSpec((T, LANES), lambda bi, c: (cur(bi, c), kcol)),
                  pl.BlockSpec((T, LANES), lambda bi, c: (nxt(bi, c), kcol)),
                  pl.BlockSpec((T, LANES), lambda bi, c: (prv(bi, c), vcol)),
                  pl.BlockSpec((T, LANES), lambda bi, c: (cur(bi, c), vcol)),
                  pl.BlockSpec((T, LANES), lambda bi, c: (nxt(bi, c), vcol)),
                  pl.BlockSpec((SWA_HEADS, T, 3 * T), lambda bi, c: (0, 0, 0)),
                  pl.BlockSpec(memory_space=pltpu.SMEM)],
        out_specs=pl.BlockSpec((T, BRANCH_WIDTH), lambda bi, c: (cur(bi, c), 0)),
        out_shape=jax.ShapeDtypeStruct((b * L, BRANCH_WIDTH), BF16),
        compiler_params=_cparams(("arbitrary", "arbitrary")),
        name="swa",
    )(u, u, u, u, u, u, u, bias, sink.astype(F32))


def _diff_kernel(q_ref, k_ref, vt_ref, bias_ref, cst_ref, lam_ref, nw_ref, o_ref,
                 qs_sc, m_sc, l_sc, acc_sc, *, nq, lam_init):
    T = DIFF_T
    h = pl.program_id(1)
    qi = pl.program_id(2)
    q = q_ref[...] * jnp.asarray(DIFF_DK ** -0.5, BF16)
    lo = lax.broadcasted_iota(jnp.int32, (T, LANES), 1) < DIFF_DK
    zero = jnp.zeros_like(q)
    qs_sc[0:T, :] = jnp.where(lo, q, zero)
    qs_sc[T:2 * T, :] = jnp.where(lo, zero, q)
    m_sc[...] = jnp.full_like(m_sc, NEG_BIG)
    l_sc[...] = jnp.zeros_like(l_sc)
    acc_sc[...] = jnp.zeros_like(acc_sc)

    def tile(kj, bias_tile, cst):
        start = pl.multiple_of(kj * T, T)
        kt = k_ref[pl.ds(start, T), :]
        vt = vt_ref[:, pl.ds(start, T)]
        s = _dot_nt(kt, qs_sc[...])
        if bias_tile is not None:
            s = s + jnp.concatenate([bias_tile, bias_tile], axis=1)
        m_old = m_sc[...]
        m_new = jnp.maximum(m_old, jnp.max(s, axis=0, keepdims=True) + cst)
        alpha = jnp.exp(m_old - m_new)
        pr = jnp.exp(s - (m_new - cst))
        l_sc[...] = alpha * l_sc[...] + jnp.sum(pr, axis=0, keepdims=True)
        acc_sc[...] = alpha * acc_sc[...] + _dot(vt, pr.astype(BF16))
        m_sc[...] = m_new

    c_before = cst_ref[h, 0]
    c_after = cst_ref[h, 1]

    def far_before(kj, carry):
        tile(kj, None, c_before)
        return carry

    def far_after(kj, carry):
        tile(kj, None, c_after)
        return carry

    lax.fori_loop(0, jnp.maximum(qi - 1, 0), far_before, 0)

    @pl.when(qi > 0)
    def _():
        tile(qi - 1, bias_ref[0, 0], 0.0)

    tile(qi, bias_ref[0, 1], 0.0)

    @pl.when(qi < nq - 1)
    def _():
        tile(qi + 1, bias_ref[0, 2], 0.0)

    lax.fori_loop(qi + 2, nq, far_after, 0)

    o = acc_sc[...] / l_sc[...]
    lp = lam_ref[...]
    lam = (jnp.exp(jnp.sum(lp[0:1] * lp[1:2], axis=-1, keepdims=True))
           - jnp.exp(jnp.sum(lp[2:3] * lp[3:4], axis=-1, keepdims=True)) + lam_init)
    out = o[:, 0:T] - lam * o[:, T:2 * T]
    out = out * lax.rsqrt(jnp.mean(out * out, axis=0, keepdims=True) + RMS_EPS)
    o_ref[...] = (out.T * nw_ref[0] * (1.0 - lam_init)).astype(BF16)


def _diff(u, b, L, lam_params, lam_init, norm_w, bias_table):
    T = DIFF_T
    nq = L // T
    d = jnp.arange(T)
    rel = (jnp.arange(3)[:, None, None] - 1) * T + d[None, :, None] - d[None, None, :]
    bias = jnp.transpose(bias_table[_t5_bucket(rel)], (3, 0, 1, 2)).astype(F32)
    far = _t5_bucket(jnp.asarray([-2 * T, 2 * T]))
    cst = bias_table[far].T.astype(F32)
    v_t = jnp.transpose(u[:, U_DV:U_DV + BRANCH_WIDTH].reshape(b, L, BRANCH_WIDTH), (0, 2, 1))
    v_t = v_t.reshape(b * BRANCH_WIDTH, L)
    return pl.pallas_call(
        functools.partial(_diff_kernel, nq=nq, lam_init=lam_init),
        grid=(b, DIFF_HEADS, nq),
        in_specs=[pl.BlockSpec((T, LANES), lambda bi, h, qi: (bi * nq + qi, U_DQ // LANES + h)),
                  pl.BlockSpec((L, LANES), lambda bi, h, qi: (bi, U_DK // LANES + h)),
                  pl.BlockSpec((DIFF_DV, L), lambda bi, h, qi: (bi * DIFF_HEADS + h, 0)),
                  pl.BlockSpec((1, 3, T, T), lambda bi, h, qi: (h, 0, 0, 0)),
                  pl.BlockSpec(memory_space=pltpu.SMEM),
                  pl.BlockSpec((4, DIFF_DK), lambda bi, h, qi: (0, 0)),
                  pl.BlockSpec((1, 1, DIFF_DV), lambda bi, h, qi: (h, 0, 0))],
        out_specs=pl.BlockSpec((T, DIFF_DV), lambda bi, h, qi: (bi * nq + qi, h)),
        out_shape=jax.ShapeDtypeStruct((b * L, BRANCH_WIDTH), BF16),
        scratch_shapes=[pltpu.VMEM((2 * T, LANES), BF16),
                        pltpu.VMEM((1, 2 * T), F32),
                        pltpu.VMEM((1, 2 * T), F32),
                        pltpu.VMEM((DIFF_DV, 2 * T), F32)],
        compiler_params=_cparams(("arbitrary", "arbitrary", "arbitrary")),
        name="diff",
    )(u, u, v_t, bias, cst, lam_params.astype(F32), norm_w.astype(F32).reshape(DIFF_HEADS, 1, DIFF_DV))


def _const_spec(shape):
    zeros = (0,) * len(shape)
    return pl.BlockSpec(shape, lambda *_: zeros, pipeline_mode=pl.Buffered(1))


def _merge_kernel(x_ref, mod_ref, nw_ref, ya_ref, yb_ref, yc_ref, yd_ref, wg_ref, bg_ref, wb_ref, wo_ref,
                  o_ref):
    x = x_ref[...]
    h = _modulate(x, nw_ref[...], mod_ref[0, 0:1, :], mod_ref[0, 1:2, :]).astype(BF16)
    merged = None
    for n, y_ref in enumerate((ya_ref, yb_ref, yc_ref, yd_ref)):
        cols = slice(n * D_MODEL, (n + 1) * D_MODEL)
        gate = jax.nn.sigmoid(_dot(h, wg_ref[:, cols]) + bg_ref[:, cols])
        term = gate * _dot(y_ref[...], wb_ref[n])
        merged = term if merged is None else merged + term
    o_ref[...] = x + mod_ref[0, 2:3, :] * _dot(merged.astype(BF16), wo_ref[...])


def _merge(x2, mod3, nw, ys, wg, bg, wb, wo, L):
    m = x2.shape[0]
    tm = 512
    per_seq = L // tm
    row = lambda i: (i, 0)
    return pl.pallas_call(
        _merge_kernel,
        grid=(m // tm,),
        in_specs=[pl.BlockSpec((tm, D_MODEL), row),
                  pl.BlockSpec((1, 6, D_MODEL), lambda i: (i // per_seq, 0, 0)),
                  _const_spec((1, D_MODEL))]
                 + [pl.BlockSpec((tm, BRANCH_WIDTH), row)] * 4
                 + [_const_spec((D_MODEL, 4 * D_MODEL)), _const_spec((1, 4 * D_MODEL)),
                    _const_spec((4, BRANCH_WIDTH, D_MODEL)), _const_spec((D_MODEL, D_MODEL))],
        out_specs=pl.BlockSpec((tm, D_MODEL), row),
        out_shape=jax.ShapeDtypeStruct((m, D_MODEL), F32),
        compiler_params=_cparams(("arbitrary",)),
        name="merge",
    )(x2, mod3, nw, *ys, wg, bg, wb, wo)


def _ffn_kernel(x_ref, mod_ref, nw_ref, w1_ref, w3_ref, w2_ref, fw_ref, o_ref, h_sc, acc_sc, *, final):
    j = pl.program_id(1)

    @pl.when(j == 0)
    def _():
        h_sc[...] = _modulate(x_ref[...], nw_ref[...], mod_ref[0, 3:4, :], mod_ref[0, 4:5, :]).astype(BF16)
        acc_sc[...] = jnp.zeros_like(acc_sc)

    h = h_sc[...]
    a = _dot(h, w1_ref[...])
    acc_sc[...] += _dot((_silu(a) * _dot(h, w3_ref[...])).astype(BF16), w2_ref[...])

    @pl.when(j == pl.num_programs(1) - 1)
    def _():
        y = x_ref[...] + mod_ref[0, 5:6, :] * acc_sc[...]
        if final:
            y = y * lax.rsqrt(jnp.mean(y * y, axis=-1, keepdims=True) + RMS_EPS) * fw_ref[...]
        o_ref[...] = y


def _ffn(x2, mod3, nw, w1, w3, w2, fw, L, final):
    m = x2.shape[0]
    tm, th = 512, 1408
    per_seq = L // tm
    return pl.pallas_call(
        functools.partial(_ffn_kernel, final=final),
        grid=(m // tm, FFN_DENSE // th),
        in_specs=[pl.BlockSpec((tm, D_MODEL), lambda i, j: (i, 0)),
                  pl.BlockSpec((1, 6, D_MODEL), lambda i, j: (i // per_seq, 0, 0)),
                  pl.BlockSpec((1, D_MODEL), lambda i, j: (0, 0)),
                  pl.BlockSpec((D_MODEL, th), lambda i, j: (0, j)),
                  pl.BlockSpec((D_MODEL, th), lambda i, j: (0, j)),
                  pl.BlockSpec((th, D_MODEL), lambda i, j: (j, 0)),
                  pl.BlockSpec((1, D_MODEL), lambda i, j: (0, 0))],
        out_specs=pl.BlockSpec((tm, D_MODEL), lambda i, j: (i, 0)),
        out_shape=jax.ShapeDtypeStruct((m, D_MODEL), F32),
        scratch_shapes=[pltpu.VMEM((tm, D_MODEL), BF16), pltpu.VMEM((tm, D_MODEL), F32)],
        compiler_params=_cparams(("arbitrary", "arbitrary")),
        name="ffn",
    )(x2, mod3, nw, w1, w3, w2, fw)


MOE_TM = 512
MOE_BLK = 512
MOE_TH = 512
MOE_NBL = 2 * N_EXPERTS
R_E0, R_E1, R_R0, R_R1, R_G0, R_G1 = range(6)
M_D0, M_D1, M_G0, M_G1 = range(4)


def _router_kernel(x_ref, mod_ref, nw_ref, wr_ref, tri_ref, h_ref, route_ref, run_sc):
    tm = MOE_TM

    @pl.when(pl.program_id(0) == 0)
    def _():
        run_sc[...] = jnp.zeros_like(run_sc)

    h2 = _modulate(x_ref[...], nw_ref[...], mod_ref[0, 3:4, :], mod_ref[0, 4:5, :])
    hi, lo = _split2(h2)
    h_ref[...] = hi
    logits = _dot(hi, wr_ref[0]) + _dot(lo, wr_ref[0]) + _dot(hi, wr_ref[1])
    lane = lax.broadcasted_iota(jnp.int32, (tm, LANES), 1).astype(F32)
    logits = jnp.where(lane < N_EXPERTS, logits, NEG_BIG)
    m1 = jnp.max(logits, axis=-1, keepdims=True)
    i1 = jnp.min(jnp.where(logits == m1, lane, float(LANES)), axis=-1, keepdims=True)
    oh1 = lane == i1
    rest = jnp.where(oh1, NEG_BIG, logits)
    m2 = jnp.max(rest, axis=-1, keepdims=True)
    i2 = jnp.min(jnp.where(rest == m2, lane, float(LANES)), axis=-1, keepdims=True)
    oh2 = lane == i2
    e = jnp.exp(m2 - m1)
    g1 = 1.0 / (1.0 + e)
    g2 = e * g1
    oh = jnp.where(oh1, 1.0, 0.0) + jnp.where(oh2, 1.0, 0.0)
    before = _dot(tri_ref[...], oh.astype(BF16)) + run_sc[...]
    r1 = jnp.sum(jnp.where(oh1, before, 0.0), axis=-1, keepdims=True)
    r2 = jnp.sum(jnp.where(oh2, before, 0.0), axis=-1, keepdims=True)
    run_sc[...] += jnp.sum(oh, axis=0, keepdims=True)
    rec = jnp.zeros((tm, LANES), F32)
    for pos, val in ((R_E0, i1), (R_E1, i2), (R_R0, r1), (R_R1, r2), (R_G0, g1), (R_G1, g2)):
        rec = jnp.where(lane == float(pos), val, rec)
    route_ref[...] = rec


def _expert_kernel(be_ref, nu_ref, tlo_ref, thi_ref, h_hbm, meta_hbm, w1_ref, w3_ref, w2_ref, o_ref,
                   x_sc, rg_sc, acc_sc, hbuf, mbuf, sem):
    del be_ref
    i = pl.program_id(0)
    j = pl.program_id(1)
    last = pl.num_programs(1) - 1
    used = i < nu_ref[0]

    def fetch(tt, slot):
        return (pltpu.make_async_copy(h_hbm.at[tt], hbuf.at[slot], sem.at[0, slot]),
                pltpu.make_async_copy(meta_hbm.at[tt], mbuf.at[slot], sem.at[1, slot]))

    @pl.when(jnp.logical_and(used, j == 0))
    def _():
        tlo = tlo_ref[i]
        thi = thi_ref[i]
        rows = (lax.broadcasted_iota(jnp.int32, (MOE_BLK, 1), 0) + i * MOE_BLK).astype(F32)
        acc_sc[...] = jnp.zeros_like(acc_sc)
        rg_sc[...] = jnp.zeros_like(rg_sc)

        @pl.when(tlo <= thi)
        def _():
            for cp in fetch(tlo, 0):
                cp.start()

        def body(tt, carry):
            slot = (tt - tlo) % 2
            for cp in fetch(tt, slot):
                cp.wait()

            @pl.when(tt < thi)
            def _():
                for cp in fetch(tt + 1, 1 - slot):
                    cp.start()

            meta = mbuf[slot]
            p0 = rows == meta[M_D0:M_D0 + 1, :]
            p1 = rows == meta[M_D1:M_D1 + 1, :]
            sel = jnp.where(p0, 1.0, jnp.where(p1, 1.0, 0.0)).astype(BF16)
            acc_sc[...] += _dot(sel, hbuf[slot])
            rg_sc[...] += jnp.sum(jnp.where(p0, meta[M_G0:M_G0 + 1, :], 0.0)
                                  + jnp.where(p1, meta[M_G1:M_G1 + 1, :], 0.0), axis=1, keepdims=True)
            return carry

        lax.fori_loop(tlo, thi + 1, body, 0)
        x_sc[...] = acc_sc[...].astype(BF16)
        acc_sc[...] = jnp.zeros_like(acc_sc)

    @pl.when(used)
    def _():
        x = x_sc[...]
        a = _dot(x, w1_ref[0])
        acc_sc[...] += _dot((_silu(a) * _dot(x, w3_ref[0])).astype(BF16), w2_ref[0])

        @pl.when(j == last)
        def _():
            o_ref[0] = (acc_sc[...] * rg_sc[...]).astype(BF16)

    @pl.when(jnp.logical_and(jnp.logical_not(used), j == last))
    def _():
        o_ref[...] = jnp.zeros_like(o_ref)


def _combine_kernel(bl_ref, x_ref, dc_ref, mod_ref, fw_ref, y_hbm, o_ref, f_sc, ybuf, sem, *, final):
    i = pl.program_id(0)

    def fetch(n):
        return pltpu.make_async_copy(y_hbm.at[bl_ref[i * MOE_NBL + n]], ybuf.at[n], sem.at[n])

    for n in range(MOE_NBL):
        @pl.when(bl_ref[i * MOE_NBL + n] >= 0)
        def _():
            fetch(n).start()

    f_sc[...] = jnp.zeros_like(f_sc)
    d0 = dc_ref[:, 0:1]
    d1 = dc_ref[:, 1:2]
    lanes = lax.broadcasted_iota(jnp.int32, (1, MOE_BLK), 1)
    for n in range(MOE_NBL):
        blk = bl_ref[i * MOE_NBL + n]

        @pl.when(blk >= 0)
        def _():
            fetch(n).wait()
            r = (lanes + blk * MOE_BLK).astype(F32)
            sel = jnp.where(d0 == r, 1.0, jnp.where(d1 == r, 1.0, 0.0)).astype(BF16)
            f_sc[...] += _dot(sel, ybuf[n])

    y = x_ref[...] + mod_ref[0, 5:6, :] * f_sc[...]
    if final:
        y = y * lax.rsqrt(jnp.mean(y * y, axis=-1, keepdims=True) + RMS_EPS) * fw_ref[...]
    o_ref[...] = y


def _moe(x2, mod3, nw, router, w1, w3, w2, fw, L, final):
    t = x2.shape[0]
    tm = MOE_TM
    per_seq = L // tm
    r_hi = jnp.zeros((D_MODEL, LANES), F32).at[:, :N_EXPERTS].set(router.astype(F32))
    r_split = jnp.stack(_split2(r_hi))
    ii = np.arange(tm)
    tri = jnp.asarray(ii[:, None] > ii[None, :], BF16)
    h2, route = pl.pallas_call(
        _router_kernel,
        grid=(t // tm,),
        in_specs=[pl.BlockSpec((tm, D_MODEL), lambda i: (i, 0)),
                  pl.BlockSpec((1, 6, D_MODEL), lambda i: (i // per_seq, 0, 0)),
                  pl.BlockSpec((1, D_MODEL), lambda i: (0, 0)),
                  pl.BlockSpec((2, D_MODEL, LANES), lambda i: (0, 0, 0)),
                  pl.BlockSpec((tm, tm), lambda i: (0, 0))],
        out_specs=[pl.BlockSpec((tm, D_MODEL), lambda i: (i, 0)),
                   pl.BlockSpec((tm, LANES), lambda i: (i, 0))],
        out_shape=[jax.ShapeDtypeStruct((t, D_MODEL), BF16),
                   jax.ShapeDtypeStruct((t, LANES), F32)],
        scratch_shapes=[pltpu.VMEM((1, LANES), F32)],
        compiler_params=_cparams(("arbitrary",)),
        name="router",
    )(x2, mod3, nw, r_split, tri)

    n_tiles = t // tm
    n_assign = t * TOP_K
    e = route[:, R_E0:R_E1 + 1].astype(jnp.int32)
    rank = route[:, R_R0:R_R1 + 1].astype(jnp.int32)
    gate = route[:, R_G0:R_G1 + 1]
    tile_cnt = jnp.sum(jax.nn.one_hot(e, N_EXPERTS, dtype=jnp.int32).reshape(n_tiles, tm * TOP_K, N_EXPERTS),
                       axis=1)
    cum = jnp.cumsum(tile_cnt, axis=0)
    cntb = jnp.concatenate([jnp.zeros((1, N_EXPERTS), jnp.int32), cum], axis=0)
    counts = cum[-1]
    padded = (counts + MOE_BLK - 1) // MOE_BLK * MOE_BLK
    pad_ends = jnp.cumsum(padded)
    pad_starts = pad_ends - padded
    dest = pad_starts[e] + rank
    n_blocks = n_assign // MOE_BLK + N_EXPERTS
    block_start = jnp.arange(n_blocks, dtype=jnp.int32) * MOE_BLK
    block_expert = jnp.minimum(jnp.sum(pad_ends[None, :] <= block_start[:, None], axis=1),
                               N_EXPERTS - 1).astype(jnp.int32)
    n_used = (pad_ends[-1:] // MOE_BLK).astype(jnp.int32)
    r_lo = block_start - pad_starts[block_expert]
    r_hi = jnp.minimum(r_lo + MOE_BLK, counts[block_expert])
    cb = cntb[:, block_expert]
    tile_lo = jnp.sum(cb[1:] <= r_lo[None, :], axis=0).astype(jnp.int32)
    tile_hi = (jnp.sum(cb[:-1] < r_hi[None, :], axis=0) - 1).astype(jnp.int32)
    s_lo = pad_starts[None, :] + cntb[:-1]
    s_hi = pad_starts[None, :] + cntb[1:]
    has = s_hi > s_lo
    b0 = s_lo // MOE_BLK
    b1 = (s_hi - 1) // MOE_BLK
    blist = jnp.stack([jnp.where(has, b0, -1), jnp.where(has & (b1 != b0), b1, -1)], axis=-1)
    blist = blist.reshape(n_tiles * MOE_NBL).astype(jnp.int32)
    destf = dest.astype(F32)
    meta = jnp.zeros((8, t), F32).at[M_D0].set(destf[:, 0]).at[M_D1].set(destf[:, 1])
    meta = meta.at[M_G0].set(gate[:, 0]).at[M_G1].set(gate[:, 1])
    meta = jnp.transpose(meta.reshape(8, n_tiles, tm), (1, 0, 2))
    dcol = jnp.zeros((t, 8), F32).at[:, :TOP_K].set(destf)

    nj = FFN_EXPERT // MOE_TH

    def jcol(i, j, nu):
        return jnp.where(i < nu[0], j, 0)

    y_rows = pl.pallas_call(
        _expert_kernel,
        grid_spec=pltpu.PrefetchScalarGridSpec(
            num_scalar_prefetch=4,
            grid=(n_blocks, nj),
            in_specs=[pl.BlockSpec(memory_space=pl.ANY),
                      pl.BlockSpec(memory_space=pl.ANY),
                      pl.BlockSpec((1, D_MODEL, MOE_TH), lambda i, j, be, nu, lo, hi: (be[i], 0, jcol(i, j, nu))),
                      pl.BlockSpec((1, D_MODEL, MOE_TH), lambda i, j, be, nu, lo, hi: (be[i], 0, jcol(i, j, nu))),
                      pl.BlockSpec((1, MOE_TH, D_MODEL), lambda i, j, be, nu, lo, hi: (be[i], jcol(i, j, nu), 0))],
            out_specs=pl.BlockSpec((1, MOE_BLK, D_MODEL), lambda i, j, be, nu, lo, hi: (i, 0, 0)),
            scratch_shapes=[pltpu.VMEM((MOE_BLK, D_MODEL), BF16),
                            pltpu.VMEM((MOE_BLK, 1), F32),
                            pltpu.VMEM((MOE_BLK, D_MODEL), F32),
                            pltpu.VMEM((2, tm, D_MODEL), BF16),
                            pltpu.VMEM((2, 8, tm), F32),
                            pltpu.SemaphoreType.DMA((2, 2))]),
        out_shape=jax.ShapeDtypeStruct((n_blocks, MOE_BLK, D_MODEL), BF16),
        compiler_params=_cparams(("arbitrary", "arbitrary")),
        name="experts",
    )(block_expert, n_used, tile_lo, tile_hi, h2.reshape(n_tiles, tm, D_MODEL), meta, w1, w3, w2)

    return pl.pallas_call(
        functools.partial(_combine_kernel, final=final),
        grid_spec=pltpu.PrefetchScalarGridSpec(
            num_scalar_prefetch=1,
            grid=(n_tiles,),
            in_specs=[pl.BlockSpec((tm, D_MODEL), lambda i, bl: (i, 0)),
                      pl.BlockSpec((tm, 8), lambda i, bl: (i, 0)),
                      pl.BlockSpec((1, 6, D_MODEL), lambda i, bl: (i // per_seq, 0, 0)),
                      pl.BlockSpec((1, D_MODEL), lambda i, bl: (0, 0)),
                      pl.BlockSpec(memory_space=pl.ANY)],
            out_specs=pl.BlockSpec((tm, D_MODEL), lambda i, bl: (i, 0)),
            scratch_shapes=[pltpu.VMEM((tm, D_MODEL), F32),
                            pltpu.VMEM((MOE_NBL, MOE_BLK, D_MODEL), BF16),
                            pltpu.SemaphoreType.DMA((MOE_NBL,))]),
        out_shape=jax.ShapeDtypeStruct((t, D_MODEL), F32),
        compiler_params=_cparams(("arbitrary",)),
        name="combine",
    )(blist, x2, dcol, mod3, fw, y_rows)


_O_SX, _O_SBC, _O_SDT, _O_CQ, _O_CK, _O_CV, _O_DQ = 2048, 2560, 2816, 2832, 3344, 3472, 3600


def _swa_pair_heads(w, axis):
    shp = w.shape
    w = w.reshape(shp[:axis] + (2, SWA_HEADS // 2, SWA_HEAD_DIM) + shp[axis + 1:])
    return jnp.swapaxes(w, axis, axis + 1).reshape(shp)


def _prep_layer(l, p):
    w_in = p['w_in'][l]
    w_u = jnp.concatenate([w_in[:, :_O_SX + 512], _swa_pair_heads(w_in[:, _O_CQ:_O_CQ + 512], 1),
                           w_in[:, _O_DQ:_O_DQ + 1536], w_in[:, _O_SBC:_O_SBC + 256],
                           w_in[:, _O_CK:_O_CK + 128], w_in[:, _O_CV:_O_CV + 128]], axis=1)
    w = {
        'w_u': w_u.astype(BF16),
        'w_dt': jnp.zeros((D_MODEL, LANES), F32).at[:, :2 * SSD_HEADS].set(
            w_in[:, _O_SDT:_O_SDT + 2 * SSD_HEADS]).astype(BF16),
        'wg': p['w_gate'][l].astype(BF16),
        'bg': p['b_gate'][l].astype(F32).reshape(1, -1),
        'wb': p['w_branch'][l].at[2].set(_swa_pair_heads(p['w_branch'][l][2], 0)).astype(BF16),
        'wo': p['w_out'][l].astype(BF16),
    }
    i = l // 2
    if l % 2 == 0:
        w['ffn'] = (p['ffn_w1'][i].astype(BF16), p['ffn_w3'][i].astype(BF16), p['ffn_w2'][i].astype(BF16))
    else:
        w['moe'] = (p['moe_router'][i], p['moe_w1'][i].astype(BF16), p['moe_w3'][i].astype(BF16),
                    p['moe_w2'][i].astype(BF16))
    return w


def _trunk(x, c, p, layers):
    b, L, _ = x.shape
    mod = _ada(c, p['w_ada'], p['b_ada'])
    x2 = x.reshape(b * L, D_MODEL)
    fw = p['final_norm_w'].astype(F32).reshape(1, D_MODEL)
    for l in range(DEPTH):
        w = layers[l]
        final = l == DEPTH - 1
        u, dt = _inproj(x2, mod[l], p['norm_mix_w'][l].reshape(1, D_MODEL), w['w_u'], w['w_dt'], L)
        ya = _retention(u, b, L, p['ret_decay_logit'][l], p['ret_norm_w'][l])
        yb = _ssd(u, dt, b, L, p['ssd_conv_w'][l], p['ssd_conv_b'][l], p['ssd_dt_bias'][l],
                  p['ssd_a_log'][l], p['ssd_d'][l], p['ssd_norm_w'][l])
        yc = _swa(u, b, L, p['swa_sink'][l], p['rel_bias'][:, :SWA_HEADS])
        yd = _diff(u, b, L, p['diff_lambda'][l], 0.8 - 0.6 * math.exp(-0.3 * l), p['diff_norm_w'][l],
                   p['rel_bias'][:, SWA_HEADS:])
        x2 = _merge(x2, mod[l], p['norm_mix_w'][l].reshape(1, D_MODEL), (ya, yb, yc, yd),
                    w['wg'], w['bg'], w['wb'], w['wo'], L)
        nw = p['norm_ffn_w'][l].reshape(1, D_MODEL)
        if l % 2 == 0:
            x2 = _ffn(x2, mod[l], nw, *w['ffn'], fw, L, final)
        else:
            x2 = _moe(x2, mod[l], nw, *w['moe'], fw, L, final)
    return x2.reshape(b, L, D_MODEL)


def kernel(x_prompt, x_sample, c_prompt, c_sample, norm_mix_w, norm_ffn_w, w_ada, b_ada, w_in,
           ret_decay_logit, ret_norm_w, ssd_conv_w, ssd_conv_b, ssd_dt_bias, ssd_a_log, ssd_d,
           ssd_norm_w, swa_sink, diff_lambda, diff_norm_w, rel_bias, w_gate, b_gate, w_branch, w_out,
           ffn_w1, ffn_w3, ffn_w2, moe_router, moe_w1, moe_w3, moe_w2, final_norm_w):
    p = {'norm_mix_w': norm_mix_w, 'norm_ffn_w': norm_ffn_w, 'w_ada': w_ada, 'b_ada': b_ada,
         'w_in': w_in, 'ret_decay_logit': ret_decay_logit, 'ret_norm_w': ret_norm_w,
         'ssd_conv_w': ssd_conv_w, 'ssd_conv_b': ssd_conv_b, 'ssd_dt_bias': ssd_dt_bias,
         'ssd_a_log': ssd_a_log, 'ssd_d': ssd_d, 'ssd_norm_w': ssd_norm_w, 'swa_sink': swa_sink,
         'diff_lambda': diff_lambda, 'diff_norm_w': diff_norm_w, 'rel_bias': rel_bias,
         'w_gate': w_gate, 'b_gate': b_gate, 'w_branch': w_branch, 'w_out': w_out,
         'ffn_w1': ffn_w1, 'ffn_w3': ffn_w3, 'ffn_w2': ffn_w2, 'moe_router': moe_router,
         'moe_w1': moe_w1, 'moe_w3': moe_w3, 'moe_w2': moe_w2, 'final_norm_w': final_norm_w}
    layers = [_prep_layer(l, p) for l in range(DEPTH)]
    return (_trunk(x_prompt, c_prompt, p, layers), _trunk(x_sample, c_sample, p, layers))
```

```python
import functools
import math

import numpy as np
import jax
import jax.numpy as jnp
from jax import lax
from jax.experimental import pallas as pl
from jax.experimental.pallas import tpu as pltpu

F32 = jnp.float32
BF16 = jnp.bfloat16

D_MODEL = 1024
DEPTH = 2
BRANCH_WIDTH = 512
RMS_EPS = 1e-6
RET_HEADS, RET_DK, RET_DV = 4, 64, 128
ROPE_BASE = 10000.0
SSD_HEADS, SSD_HEADDIM, SSD_GROUPS, SSD_STATE, SSD_CONV_K = 8, 64, 2, 64, 5
SWA_HEADS, SWA_KV_HEADS, SWA_HEAD_DIM, WINDOW = 8, 2, 64, 128
DIFF_HEADS, DIFF_DK, DIFF_DV = 4, 64, 128
NUM_BUCKETS, MAX_DISTANCE = 32, 128
FFN_DENSE = 2816
N_EXPERTS, TOP_K, FFN_EXPERT = 8, 2, 3584
LANES = 128
NEG_BIG = -1e30
LOG2E = math.log2(math.e)

U_RQ, U_RK, U_RV, U_RG = 0, 256, 512, 1024
U_SZ, U_SX = 1536, 2048
U_CQ, U_DQ, U_DK, U_DV = 2560, 3072, 3584, 4096
U_SBC, U_CK, U_CV = 4608, 4864, 4992
U_WIDTH = 5120

RET_T = 256
SSD_T = 128
SWA_T = 128
DIFF_T = 512
VMEM_LIMIT = 56 * 1024 * 1024


def _cparams(sem):
    return pltpu.CompilerParams(dimension_semantics=sem, vmem_limit_bytes=VMEM_LIMIT)


def _dot(a, b):
    return jnp.dot(a, b, preferred_element_type=F32)


def _dot_nt(a, b):
    return lax.dot_general(a, b, (((1,), (1,)), ((), ())), preferred_element_type=F32)


def _dot_tn(a, b):
    return lax.dot_general(a, b, (((0,), (0,)), ((), ())), preferred_element_type=F32)


def _silu(x):
    return x * jax.nn.sigmoid(x)


def _split2(x):
    hi = x.astype(BF16)
    lo = (x - hi.astype(F32)).astype(BF16)
    return hi, lo


def _split3(x):
    hi = x.astype(BF16)
    r = x - hi.astype(F32)
    mid = r.astype(BF16)
    lo = (r - mid.astype(F32)).astype(BF16)
    return hi, mid, lo


def _modulate(x, nw, shift, scale):
    y = x * lax.rsqrt(jnp.mean(x * x, axis=-1, keepdims=True) + RMS_EPS) * nw
    return y * (1.0 + scale) + shift


def _ada_kernel(c_ref, w_ref, b_ref, o_ref):
    s = _silu(c_ref[...]).astype(BF16)
    o_ref[0] = _dot(s, w_ref[0].astype(BF16)) + b_ref[0]


def _ada(c, w_ada, b_ada):
    b = c.shape[0]
    cp = jnp.zeros((8, D_MODEL), F32).at[:b].set(c)
    tn = 1536
    out = pl.pallas_call(
        _ada_kernel,
        grid=(DEPTH, 6 * D_MODEL // tn),
        in_specs=[pl.BlockSpec((8, D_MODEL), lambda l, j: (0, 0)),
                  pl.BlockSpec((1, D_MODEL, tn), lambda l, j: (l, 0, j)),
                  pl.BlockSpec((1, 1, tn), lambda l, j: (l, 0, j))],
        out_specs=pl.BlockSpec((1, 8, tn), lambda l, j: (l, 0, j)),
        out_shape=jax.ShapeDtypeStruct((DEPTH, 8, 6 * D_MODEL), F32),
        compiler_params=_cparams(("arbitrary", "arbitrary")),
        name="ada",
    )(cp, w_ada, b_ada.reshape(DEPTH, 1, 6 * D_MODEL))
    return out[:, :b].reshape(DEPTH, b, 6, D_MODEL)


def _inproj_kernel(x_ref, mod_ref, nw_ref, w_ref, wdt_ref, u_ref, dt_ref, h_sc):
    @pl.when(pl.program_id(1) == 0)
    def _():
        h = _modulate(x_ref[...], nw_ref[...], mod_ref[0, 0:1, :], mod_ref[0, 1:2, :]).astype(BF16)
        h_sc[...] = h
        dt_ref[...] = _dot(h, wdt_ref[...])

    u_ref[...] = _dot(h_sc[...], w_ref[...]).astype(BF16)


def _inproj(x2, mod3, nw, w_u, w_dt, L):
    m = x2.shape[0]
    tm, tn = 1024, 512
    per_seq = L // tm
    return pl.pallas_call(
        _inproj_kernel,
        grid=(m // tm, U_WIDTH // tn),
        in_specs=[pl.BlockSpec((tm, D_MODEL), lambda i, j: (i, 0)),
                  pl.BlockSpec((1, 6, D_MODEL), lambda i, j: (i // per_seq, 0, 0)),
                  pl.BlockSpec((1, D_MODEL), lambda i, j: (0, 0)),
                  pl.BlockSpec((D_MODEL, tn), lambda i, j: (0, j)),
                  pl.BlockSpec((D_MODEL, LANES), lambda i, j: (0, 0))],
        out_specs=[pl.BlockSpec((tm, tn), lambda i, j: (i, j)),
                   pl.BlockSpec((tm, LANES), lambda i, j: (i, 0))],
        out_shape=[jax.ShapeDtypeStruct((m, U_WIDTH), BF16),
                   jax.ShapeDtypeStruct((m, LANES), F32)],
        scratch_shapes=[pltpu.VMEM((tm, D_MODEL), BF16)],
        compiler_params=_cparams(("arbitrary", "arbitrary")),
        name="inproj",
    )(x2, mod3, nw, w_u, w_dt)


def _ret_kernel(q_ref, k_ref, v_ref, g_ref, cos_ref, sin_ref, dbi_ref, gq_ref, gk_ref, dec_ref,
                nw_ref, o_ref, sf_sc, sb_sc, sbst_sc, *, nc):
    T = RET_T
    p = pl.program_id(1)
    c = pl.program_id(2)
    cos = cos_ref[...]
    sin = sin_ref[...]
    lane = lax.broadcasted_iota(jnp.int32, (T, 2 * LANES), 1)
    first_half = (lane % RET_DK) < (RET_DK // 2)
    lo_head = lax.broadcasted_iota(jnp.int32, (T, LANES), 1) < RET_DK

    def rope(ref):
        x = ref[...].astype(F32)
        rot = jnp.where(first_half, pltpu.roll(x, 2 * LANES - RET_DK // 2, 1),
                        pltpu.roll(x, RET_DK // 2, 1))
        return x * cos + rot * sin

    def head_masked(x, h):
        t = x[:, (h // 2) * LANES:(h // 2 + 1) * LANES]
        keep = lo_head if h % 2 == 0 else jnp.logical_not(lo_head)
        return jnp.where(keep, t, jnp.zeros_like(t))

    @pl.when(p == 0)
    def _():
        @pl.when(c == 0)
        def _():
            sb_sc[...] = jnp.zeros_like(sb_sc)

        kb = (rope(k_ref) * (RET_DK ** -0.5) * gk_ref[1]).astype(BF16)
        v = v_ref[...]
        cc = nc - 1 - c
        for t in range(RET_HEADS // 2):
            sb = sb_sc[t]
            sbst_sc[cc, t] = sb.astype(BF16)
            upd = (_dot_tn(head_masked(kb, 2 * t), v[:, (2 * t) * RET_DV:(2 * t + 1) * RET_DV])
                   + _dot_tn(head_masked(kb, 2 * t + 1), v[:, (2 * t + 1) * RET_DV:(2 * t + 2) * RET_DV]))
            sb_sc[t] = sb * dec_ref[1, t] + upd

    @pl.when(p == 1)
    def _():
        @pl.when(c == 0)
        def _():
            sf_sc[...] = jnp.zeros_like(sf_sc)

        qr = rope(q_ref)
        kr = rope(k_ref) * (RET_DK ** -0.5)
        qb = qr.astype(BF16)
        kb = kr.astype(BF16)
        qf = (qr * gq_ref[0]).astype(BF16)
        qw = (qr * gq_ref[1]).astype(BF16)
        kf = (kr * gk_ref[0]).astype(BF16)
        v = v_ref[...]
        ys = []
        upds = []
        for h in range(RET_HEADS):
            t = h // 2
            vh = v[:, h * RET_DV:(h + 1) * RET_DV]
            kt = kb[:, t * LANES:(t + 1) * LANES]
            s = _dot_nt(head_masked(qb, h), kt)
            y = _dot((s * dbi_ref[h]).astype(BF16), vh)
            y = y + _dot(head_masked(qf, h), sf_sc[t].astype(BF16))
            y = y + _dot(head_masked(qw, h), sbst_sc[c, t])
            upds.append(_dot_tn(head_masked(kf, h), vh))
            mu = jnp.mean(y, axis=-1, keepdims=True)
            yc = y - mu
            var = jnp.mean(yc * yc, axis=-1, keepdims=True)
            ys.append(yc * lax.rsqrt(var + RMS_EPS))
        for t in range(RET_HEADS // 2):
            sf_sc[t] = sf_sc[t] * dec_ref[0, t] + upds[2 * t] + upds[2 * t + 1]
        y = jnp.concatenate(ys, axis=1) * nw_ref[...]
        g = g_ref[...].astype(F32)
        o_ref[...] = (_silu(g) * y).astype(BF16)


def _ret_tables(decay_logit):
    T = RET_T
    lg = jax.nn.log_sigmoid(decay_logit.astype(F32))
    i = jnp.arange(T, dtype=F32)
    diff = i[:, None] - i[None, :]
    dbi = jnp.where(diff[None] >= 0, jnp.exp(lg[0][:, None, None] * diff[None]),
                    jnp.exp(-lg[1][:, None, None] * diff[None]))

    def cols(per_head):
        return jnp.repeat(per_head.T, RET_DK, axis=1)

    gq = jnp.stack([cols(jnp.exp(lg[0][:, None] * (i + 1)[None])),
                    cols(jnp.exp(lg[1][:, None] * (T - i)[None]))])
    gk = jnp.stack([cols(jnp.exp(lg[0][:, None] * (T - 1 - i)[None])),
                    cols(jnp.exp(lg[1][:, None] * i[None]))])
    dec = jnp.exp(lg * T)
    dec = jnp.repeat(dec.reshape(2, RET_HEADS // 2, 2, 1), RET_DK, axis=2).reshape(2, RET_HEADS // 2, LANES, 1)
    dec = jnp.broadcast_to(dec, (2, RET_HEADS // 2, LANES, RET_DV))
    return dbi, gq, gk, dec


def _rope_tables(L):
    half = RET_DK // 2
    inv = ROPE_BASE ** (-jnp.arange(half, dtype=F32) / half)
    ang = jnp.arange(L, dtype=F32)[:, None] * inv[None, :]
    cos = jnp.tile(jnp.concatenate([jnp.cos(ang), jnp.cos(ang)], axis=1), (1, RET_HEADS))
    sin = jnp.tile(jnp.concatenate([-jnp.sin(ang), jnp.sin(ang)], axis=1), (1, RET_HEADS))
    return cos, sin


def _retention(u, b, L, decay_logit, norm_w):
    T = RET_T
    nc = L // T
    dbi, gq, gk, dec = _ret_tables(decay_logit)
    cos, sin = _rope_tables(L)

    def rows(bi, p, c):
        return bi * nc + jnp.where(p == 0, nc - 1 - c, c)

    def rows_p1(bi, p, c):
        return bi * nc + jnp.where(p == 0, 0, c)

    def pos(bi, p, c):
        return jnp.where(p == 0, nc - 1 - c, c)

    const2 = lambda bi, p, c: (0, 0)
    const3 = lambda bi, p, c: (0, 0, 0)
    const4 = lambda bi, p, c: (0, 0, 0, 0)
    return pl.pallas_call(
        functools.partial(_ret_kernel, nc=nc),
        grid=(b, 2, nc),
        in_specs=[pl.BlockSpec((T, 256), lambda bi, p, c: (rows_p1(bi, p, c), U_RQ // 256)),
                  pl.BlockSpec((T, 256), lambda bi, p, c: (rows(bi, p, c), U_RK // 256)),
                  pl.BlockSpec((T, 512), lambda bi, p, c: (rows(bi, p, c), U_RV // 512)),
                  pl.BlockSpec((T, 512), lambda bi, p, c: (rows_p1(bi, p, c), U_RG // 512)),
                  pl.BlockSpec((T, 256), lambda bi, p, c: (pos(bi, p, c), 0)),
                  pl.BlockSpec((T, 256), lambda bi, p, c: (pos(bi, p, c), 0)),
                  pl.BlockSpec((RET_HEADS, T, T), const3),
                  pl.BlockSpec((2, T, 256), const3),
                  pl.BlockSpec((2, T, 256), const3),
                  pl.BlockSpec((2, RET_HEADS // 2, LANES, RET_DV), const4),
                  pl.BlockSpec((1, BRANCH_WIDTH), const2)],
        out_specs=pl.BlockSpec((T, BRANCH_WIDTH), lambda bi, p, c: (rows_p1(bi, p, c), 0)),
        out_shape=jax.ShapeDtypeStruct((b * L, BRANCH_WIDTH), BF16),
        scratch_shapes=[pltpu.VMEM((RET_HEADS // 2, LANES, RET_DV), F32),
                        pltpu.VMEM((RET_HEADS // 2, LANES, RET_DV), F32),
                        pltpu.VMEM((nc, RET_HEADS // 2, LANES, RET_DV), BF16)],
        compiler_params=_cparams(("arbitrary", "arbitrary", "arbitrary")),
        name="retention",
    )(u, u, u, u, cos, sin, dbi, gq, gk, dec, norm_w.reshape(1, BRANCH_WIDTH))


SSD_HALO = 16


def _ssd_kernel(z_ref, x_ref, xp_ref, xn_ref, bc_ref, bcp_ref, bcn_ref, dt_ref,
                cwx_ref, cwbc_ref, cbx_ref, cbbc_ref, dtb_ref, a_ref, dsk_ref, nw_ref,
                tri_ref, trit_ref, ef_ref, eb_ref, o_ref, sf_sc, sb_sc, sbst_sc, *, nc):
    T = SSD_T
    H = SSD_HEADS
    GW = BRANCH_WIDTH // SSD_GROUPS
    p = pl.program_id(1)
    c = pl.program_id(2)
    cc = jnp.where(p == 0, nc - 1 - c, c)
    prev_ok = (cc > 0).astype(F32)
    next_ok = (cc < nc - 1).astype(F32)

    def conv_silu(cur_ref, prev_ref, next_ref, w_ref, b_ref):
        cur = cur_ref[...].astype(F32)
        prev = prev_ref[...].astype(F32) * prev_ok
        nxt = next_ref[...].astype(F32) * next_ok
        row = lax.broadcasted_iota(jnp.int32, cur.shape, 0)
        acc = cur * w_ref[2:3, :] + b_ref[...]
        s = jnp.where(row == 0, prev[SSD_HALO - 1:SSD_HALO], pltpu.roll(cur, 1, 0))
        acc = acc + s * w_ref[1:2, :]
        s = jnp.where(row == 0, prev[SSD_HALO - 2:SSD_HALO - 1],
                      jnp.where(row == 1, prev[SSD_HALO - 1:SSD_HALO], pltpu.roll(cur, 2, 0)))
        acc = acc + s * w_ref[0:1, :]
        s = jnp.where(row == T - 1, nxt[0:1], pltpu.roll(cur, T - 1, 0))
        acc = acc + s * w_ref[3:4, :]
        s = jnp.where(row == T - 2, nxt[0:1],
                      jnp.where(row == T - 1, nxt[1:2], pltpu.roll(cur, T - 2, 0)))
        acc = acc + s * w_ref[4:5, :]
        return _silu(acc)

    def cum(mat, x):
        hi, mid, lo = _split3(x)
        return _dot(mat, hi) + _dot(mat, mid) + _dot(mat, lo)

    def cum_r(x, mat):
        hi, mid, lo = _split3(x)
        return _dot(hi, mat) + _dot(mid, mat) + _dot(lo, mat)

    def expand(x, e_ref):
        hi, lo = _split2(x)
        return _dot(hi, e_ref[...]) + _dot(lo, e_ref[...])

    lane = lax.broadcasted_iota(jnp.int32, (T, LANES), 1)
    is_fwd = lane < H
    xraw = dt_ref[...] + dtb_ref[...]
    dt = jnp.maximum(xraw, 0.0) + jnp.log(1.0 + jnp.exp(-jnp.abs(xraw)))
    dt = jnp.where(lane < 2 * H, dt, 0.0)
    la = dt * a_ref[...]
    tri = tri_ref[...]
    trit = trit_ref[...]
    xs = conv_silu(x_ref, xp_ref, xn_ref, cwx_ref, cbx_ref)
    bcm = conv_silu(bc_ref, bcp_ref, bcn_ref, cwbc_ref, cbbc_ref)
    bm = bcm[:, :LANES].astype(BF16)
    cm = bcm[:, LANES:].astype(BF16)
    lo_grp = lane < SSD_STATE

    def grp_masked(x, g):
        keep = lo_grp if g == 0 else jnp.logical_not(lo_grp)
        return jnp.where(keep, x, jnp.zeros_like(x))

    srow = lax.broadcasted_iota(jnp.int32, (2 * SSD_STATE, BRANCH_WIDTH), 0)
    scol = lax.broadcasted_iota(jnp.int32, (2 * SSD_STATE, BRANCH_WIDTH), 1)
    blockdiag = (srow < SSD_STATE) == (scol < GW)
    rb = cum(trit, la)

    @pl.when(p == 0)
    def _():
        @pl.when(c == 0)
        def _():
            sb_sc[...] = jnp.zeros_like(sb_sc)

        wall = jnp.exp(rb[0:1, :] - rb) * dt
        wb_x = expand(wall, eb_ref)
        dec_x = expand(jnp.exp(rb[0:8, :]), eb_ref)[0:1, :]
        xw = (xs * wb_x).astype(BF16)
        sb = sb_sc[...]
        sbst_sc[cc] = jnp.concatenate([sb[:SSD_STATE, :GW], sb[SSD_STATE:, GW:]], axis=0).astype(BF16)
        sb_sc[...] = sb * dec_x + jnp.where(blockdiag, _dot_tn(bm, xw), 0.0)

    @pl.when(p == 1)
    def _():
        @pl.when(c == 0)
        def _():
            sf_sc[...] = jnp.zeros_like(sf_sc)

        cf = cum(tri, la)
        la_t = la.T[0:2 * H]
        dt_t = dt.T[0:2 * H]
        rf_row = cum_r(la_t, trit)
        rb_row = cum_r(la_t, tri)
        vall = jnp.exp(jnp.where(is_fwd, cf, rb))
        wall = jnp.exp(jnp.where(is_fwd, cf[T - 1:T, :] - cf, 0.0)) * dt
        ef_x = expand(vall, ef_ref)
        eb_x = expand(vall, eb_ref)
        wf_x = expand(wall, ef_ref)
        ii = lax.broadcasted_iota(jnp.int32, (T, T), 0)
        jj = lax.broadcasted_iota(jnp.int32, (T, T), 1)
        lower = jj <= ii
        xsb = xs.astype(BF16)
        y_tiles = []
        for g in range(SSD_GROUPS):
            gmat = _dot_nt(grp_masked(cm, g), bm)
            for pair in range(2):
                tile_idx = 2 * g + pair
                xt = xsb[:, tile_idx * LANES:(tile_idx + 1) * LANES]
                acc = None
                for sub in range(2):
                    h = 2 * tile_idx + sub
                    seg = jnp.where(lower, cf[:, h:h + 1] - rf_row[h:h + 1, :],
                                    rb[:, H + h:H + h + 1] - rb_row[H + h:H + h + 1, :])
                    dsel = jnp.where(jj < ii, dt_t[h:h + 1, :],
                                     jnp.where(jj > ii, dt_t[H + h:H + h + 1, :],
                                               dt_t[h:h + 1, :] + dt_t[H + h:H + h + 1, :]))
                    wm = (gmat * jnp.exp(seg) * dsel).astype(BF16)
                    part = _dot(wm, grp_masked(xt, sub))
                    acc = part if acc is None else acc + part
                y_tiles.append(acc)
        y = jnp.concatenate(y_tiles, axis=1)
        sf = sf_sc[...]
        ycf = _dot(cm, sf.astype(BF16))
        st = sbst_sc[c]
        zero = jnp.zeros((SSD_STATE, GW), BF16)
        sb_next = jnp.concatenate([jnp.concatenate([st[:SSD_STATE], zero], axis=1),
                                   jnp.concatenate([zero, st[SSD_STATE:]], axis=1)], axis=0)
        ycb = _dot(cm, sb_next)
        xwf = (xs * wf_x).astype(BF16)
        sf_sc[...] = sf * ef_x[T - 1:T, :] + jnp.where(blockdiag, _dot_tn(bm, xwf), 0.0)
        y = y + ycf * ef_x + ycb * eb_x + xs * dsk_ref[...]
        z = z_ref[...].astype(F32)
        y = y * _silu(z)
        outs = []
        for g in range(SSD_GROUPS):
            yg = y[:, g * GW:(g + 1) * GW]
            outs.append(yg * lax.rsqrt(jnp.mean(yg * yg, axis=-1, keepdims=True) + RMS_EPS))
        o_ref[...] = (jnp.concatenate(outs, axis=1) * nw_ref[...]).astype(BF16)


def _ssd(u, dt_raw, b, L, conv_w, conv_b, dt_bias, a_log, d_skip, norm_w):
    T = SSD_T
    H = SSD_HEADS
    nc = L // T
    hb = T // SSD_HALO
    n_halo = b * L // SSD_HALO

    def chunk(p, c):
        return jnp.where(p == 0, nc - 1 - c, c)

    def rows(bi, p, c):
        return bi * nc + chunk(p, c)

    def rows_p1(bi, p, c):
        return bi * nc + jnp.where(p == 0, 0, c)

    def prev_rows(bi, p, c):
        return jnp.maximum(rows(bi, p, c) * hb - 1, 0)

    def next_rows(bi, p, c):
        return jnp.minimum((rows(bi, p, c) + 1) * hb, n_halo - 1)

    i = np.arange(T)
    tri = jnp.asarray(i[:, None] >= i[None, :], BF16)
    trit = jnp.asarray(i[:, None] <= i[None, :], BF16)
    hcol = np.arange(BRANCH_WIDTH) // SSD_HEADDIM
    ef = jnp.asarray(np.arange(LANES)[:, None] == hcol[None, :], BF16)
    eb = jnp.asarray(np.arange(LANES)[:, None] == (hcol[None, :] + H), BF16)
    a = -jnp.exp(a_log.astype(F32)).reshape(1, 2 * H)
    a_row = jnp.zeros((1, LANES), F32).at[:, :2 * H].set(a)
    dtb = jnp.zeros((1, LANES), F32).at[:, :2 * H].set(dt_bias.astype(F32).reshape(1, 2 * H))
    dsk = jnp.repeat(d_skip.astype(F32), SSD_HEADDIM).reshape(1, BRANCH_WIDTH)
    cw = conv_w.astype(F32)
    cb = conv_b.astype(F32).reshape(1, -1)
    const2 = lambda bi, p, c: (0, 0)
    return pl.pallas_call(
        functools.partial(_ssd_kernel, nc=nc),
        grid=(b, 2, nc),
        in_specs=[pl.BlockSpec((T, 512), lambda bi, p, c: (rows_p1(bi, p, c), U_SZ // 512)),
                  pl.BlockSpec((T, 512), lambda bi, p, c: (rows(bi, p, c), U_SX // 512)),
                  pl.BlockSpec((SSD_HALO, 512), lambda bi, p, c: (prev_rows(bi, p, c), U_SX // 512)),
                  pl.BlockSpec((SSD_HALO, 512), lambda bi, p, c: (next_rows(bi, p, c), U_SX // 512)),
                  pl.BlockSpec((T, 256), lambda bi, p, c: (rows(bi, p, c), U_SBC // 256)),
                  pl.BlockSpec((SSD_HALO, 256), lambda bi, p, c: (prev_rows(bi, p, c), U_SBC // 256)),
                  pl.BlockSpec((SSD_HALO, 256), lambda bi, p, c: (next_rows(bi, p, c), U_SBC // 256)),
                  pl.BlockSpec((T, LANES), lambda bi, p, c: (rows(bi, p, c), 0)),
                  pl.BlockSpec((SSD_CONV_K, 512), const2),
                  pl.BlockSpec((SSD_CONV_K, 256), const2),
                  pl.BlockSpec((1, 512), const2),
                  pl.BlockSpec((1, 256), const2),
                  pl.BlockSpec((1, LANES), const2),
                  pl.BlockSpec((1, LANES), const2),
                  pl.BlockSpec((1, BRANCH_WIDTH), const2),
                  pl.BlockSpec((1, BRANCH_WIDTH), const2),
                  pl.BlockSpec((T, T), const2),
                  pl.BlockSpec((T, T), const2),
                  pl.BlockSpec((LANES, BRANCH_WIDTH), const2),
                  pl.BlockSpec((LANES, BRANCH_WIDTH), const2)],
        out_specs=pl.BlockSpec((T, BRANCH_WIDTH), lambda bi, p, c: (rows_p1(bi, p, c), 0)),
        out_shape=jax.ShapeDtypeStruct((b * L, BRANCH_WIDTH), BF16),
        scratch_shapes=[pltpu.VMEM((2 * SSD_STATE, BRANCH_WIDTH), F32),
                        pltpu.VMEM((2 * SSD_STATE, BRANCH_WIDTH), F32),
                        pltpu.VMEM((nc, 2 * SSD_STATE, BRANCH_WIDTH // SSD_GROUPS), BF16)],
        compiler_params=_cparams(("arbitrary", "arbitrary", "arbitrary")),
        name="ssd",
    )(u, u, u, u, u, u, u, dt_raw, cw[:, :512], cw[:, 512:], cb[:, :512], cb[:, 512:], dtb, a_row,
      dsk, norm_w.reshape(1, BRANCH_WIDTH), tri, trit, ef, eb)


def _t5_bucket(rel):
    half = NUM_BUCKETS // 2
    max_exact = half // 2
    n = jnp.abs(rel)
    sign = jnp.where(rel > 0, half, 0)
    large = max_exact + (jnp.log(jnp.maximum(n, 1).astype(F32) / max_exact)
                         / math.log(MAX_DISTANCE / max_exact) * (half - max_exact)).astype(jnp.int32)
    large = jnp.minimum(large, half - 1)
    return sign + jnp.where(n < max_exact, n, large)


def _swa_kernel(q_ref, kp_ref, kc_ref, kn_ref, vp_ref, vc_ref, vn_ref, bias_ref, sink_ref, o_ref, *, nb):
    T = SWA_T
    c = pl.program_id(1)
    q = q_ref[...] * jnp.asarray(SWA_HEAD_DIM ** -0.5, BF16)
    k = jnp.concatenate([kp_ref[...], kc_ref[...], kn_ref[...]], axis=0)
    v = jnp.concatenate([vp_ref[...], vc_ref[...], vn_ref[...]], axis=0)
    col = lax.broadcasted_iota(jnp.int32, (T, 3 * T), 1)
    edge = jnp.where(col < T, jnp.where(c > 0, 0.0, NEG_BIG),
                     jnp.where(col >= 2 * T, jnp.where(c < nb - 1, 0.0, NEG_BIG), 0.0))
    lo = lax.broadcasted_iota(jnp.int32, (T, LANES), 1) < SWA_HEAD_DIM
    tiles = []
    for t in range(SWA_HEADS // 2):
        qt = q[:, t * LANES:(t + 1) * LANES]
        halves = []
        for half in range(2):
            h = t + (SWA_HEADS // 2) * half
            keep = lo if half == 0 else jnp.logical_not(lo)
            s = _dot_nt(jnp.where(keep, qt, jnp.zeros_like(qt)), k) + bias_ref[h] + edge
            sk = sink_ref[h]
            m = jnp.maximum(jnp.max(s, axis=-1, keepdims=True), sk)
            pr = jnp.exp(s - m)
            den = jnp.sum(pr, axis=-1, keepdims=True) + jnp.exp(sk - m)
            halves.append(_dot(pr.astype(BF16), v) / den)
        tiles.append(jnp.where(lo, halves[0], halves[1]))
    o_ref[...] = jnp.concatenate(tiles, axis=1).astype(BF16)


def _swa(u, b, L, sink, bias_table):
    T = SWA_T
    nb = L // T
    n_blk = b * nb
    krel = jnp.arange(3 * T) - T
    rel = krel[None, :] - jnp.arange(T)[:, None]
    bias = jnp.transpose(bias_table[_t5_bucket(rel)], (2, 0, 1)).astype(F32)
    bias = jnp.where((jnp.abs(rel) <= WINDOW)[None], bias, NEG_BIG)

    def cur(bi, c):
        return bi * nb + c

    def prv(bi, c):
        return jnp.maximum(bi * nb + c - 1, 0)

    def nxt(bi, c):
        return jnp.minimum(bi * nb + c + 1, n_blk - 1)

    kcol, vcol = U_CK // LANES, U_CV // LANES
    return pl.pallas_call(
        functools.partial(_swa_kernel, nb=nb),
        grid=(b, nb),
        in_specs=[pl.BlockSpec((T, 512), lambda bi, c: (cur(bi, c), U_CQ // 512)),
                  pl.BlockSpec((T, LANES), lambda bi, c: (prv(bi, c), kcol)),
                  pl.BlockSpec((T, LANES), lambda bi, c: (cur(bi, c), kcol)),
                  pl.BlockSpec((T, LANES), lambda bi, c: (nxt(bi, c), kcol)),
                  pl.BlockSpec((T, LANES), lambda bi, c: (prv(bi, c), vcol)),
                  pl.BlockSpec((T, LANES), lambda bi, c: (cur(bi, c), vcol)),
                  pl.BlockSpec((T, LANES), lambda bi, c: (nxt(bi, c), vcol)),
                  pl.BlockSpec((SWA_HEADS, T, 3 * T), lambda bi, c: (0, 0, 0)),
                  pl.BlockSpec(memory_space=pltpu.SMEM)],
        out_specs=pl.BlockSpec((T, BRANCH_WIDTH), lambda bi, c: (cur(bi, c), 0)),
        out_shape=jax.ShapeDtypeStruct((b * L, BRANCH_WIDTH), BF16),
        compiler_params=_cparams(("arbitrary", "arbitrary")),
        name="swa",
    )(u, u, u, u, u, u, u, bias, sink.astype(F32))


def _diff_kernel(q_ref, k_ref, vt_ref, bias_ref, cst_ref, lam_ref, nw_ref, o_ref,
                 qs_sc, m_sc, l_sc, acc_sc, *, nq, lam_init):
    T = DIFF_T
    h = pl.program_id(1)
    qi = pl.program_id(2)
    q = (q_ref[...].astype(F32) * (DIFF_DK ** -0.5 * LOG2E)).astype(BF16)
    lo = lax.broadcasted_iota(jnp.int32, (T, LANES), 1) < DIFF_DK
    zero = jnp.zeros_like(q)
    qs_sc[0:T, :] = jnp.where(lo, q, zero)
    qs_sc[T:2 * T, :] = jnp.where(lo, zero, q)
    m_sc[...] = jnp.full_like(m_sc, NEG_BIG)
    l_sc[...] = jnp.zeros_like(l_sc)
    acc_sc[...] = jnp.zeros_like(acc_sc)

    def tile(kj, bias_tile, cst):
        start = pl.multiple_of(kj * T, T)
        kt = k_ref[pl.ds(start, T), :]
        vt = vt_ref[:, pl.ds(start, T)]
        s = _dot_nt(kt, qs_sc[...])
        if bias_tile is not None:
            s = s + jnp.concatenate([bias_tile, bias_tile], axis=1)
        m_old = m_sc[...]
        m_new = jnp.maximum(m_old, jnp.max(s, axis=0, keepdims=True) + cst)
        alpha = jnp.exp2(m_old - m_new)
        pr = jnp.exp2(s - (m_new - cst))
        l_sc[...] = alpha * l_sc[...] + jnp.sum(pr, axis=0, keepdims=True)
        acc_sc[...] = alpha * acc_sc[...] + _dot(vt, pr.astype(BF16))
        m_sc[...] = m_new

    c_before = cst_ref[h, 0]
    c_after = cst_ref[h, 1]

    def far_before(kj, carry):
        tile(kj, None, c_before)
        return carry

    def far_after(kj, carry):
        tile(kj, None, c_after)
        return carry

    lax.fori_loop(0, jnp.maximum(qi - 1, 0), far_before, 0)

    @pl.when(qi > 0)
    def _():
        tile(qi - 1, bias_ref[0, 0], 0.0)

    tile(qi, bias_ref[0, 1], 0.0)

    @pl.when(qi < nq - 1)
    def _():
        tile(qi + 1, bias_ref[0, 2], 0.0)

    lax.fori_loop(qi + 2, nq, far_after, 0)

    o = acc_sc[...] / l_sc[...]
    lp = lam_ref[...]
    lam = (jnp.exp(jnp.sum(lp[0:1] * lp[1:2], axis=-1, keepdims=True))
           - jnp.exp(jnp.sum(lp[2:3] * lp[3:4], axis=-1, keepdims=True)) + lam_init)
    out = o[:, 0:T] - lam * o[:, T:2 * T]
    out = out * lax.rsqrt(jnp.mean(out * out, axis=0, keepdims=True) + RMS_EPS)
    o_ref[...] = (out.T * nw_ref[0] * (1.0 - lam_init)).astype(BF16)


def _diff(u, b, L, lam_params, lam_init, norm_w, bias_table):
    T = DIFF_T
    nq = L // T
    d = jnp.arange(T)
    rel = (jnp.arange(3)[:, None, None] - 1) * T + d[None, :, None] - d[None, None, :]
    bucket = _t5_bucket(rel)
    table = bias_table.astype(F32) * LOG2E
    bias = jnp.zeros((DIFF_HEADS,) + rel.shape, F32)
    for bkt in range(NUM_BUCKETS):
        bias = jnp.where((bucket == bkt)[None], table[bkt][:, None, None, None], bias)
    far = _t5_bucket(jnp.asarray([-2 * T, 2 * T]))
    cst = table[far].T
    v_t =jnp.transpose(u[:, U_DV:U_DV + BRANCH_WIDTH].reshape(b, L, BRANCH_WIDTH), (0, 2, 1))
    v_t = v_t.reshape(b * BRANCH_WIDTH, L)
    return pl.pallas_call(
        functools.partial(_diff_kernel, nq=nq, lam_init=lam_init),
        grid=(b, DIFF_HEADS, nq),
        in_specs=[pl.BlockSpec((T, LANES), lambda bi, h, qi: (bi * nq + qi, U_DQ // LANES + h)),
                  pl.BlockSpec((L, LANES), lambda bi, h, qi: (bi, U_DK // LANES + h)),
                  pl.BlockSpec((DIFF_DV, L), lambda bi, h, qi: (bi * DIFF_HEADS + h, 0)),
                  pl.BlockSpec((1, 3, T, T), lambda bi, h, qi: (h, 0, 0, 0)),
                  pl.BlockSpec(memory_space=pltpu.SMEM),
                  pl.BlockSpec((4, DIFF_DK), lambda bi, h, qi: (0, 0)),
                  pl.BlockSpec((1, 1, DIFF_DV), lambda bi, h, qi: (h, 0, 0))],
        out_specs=pl.BlockSpec((T, DIFF_DV), lambda bi, h, qi: (bi * nq + qi, h)),
        out_shape=jax.ShapeDtypeStruct((b * L, BRANCH_WIDTH), BF16),
        scratch_shapes=[pltpu.VMEM((2 * T, LANES), BF16),
                        pltpu.VMEM((1, 2 * T), F32),
                        pltpu.VMEM((1, 2 * T), F32),
                        pltpu.VMEM((DIFF_DV, 2 * T), F32)],
        compiler_params=_cparams(("arbitrary", "arbitrary", "arbitrary")),
        name="diff",
    )(u, u, v_t, bias, cst, lam_params.astype(F32), norm_w.astype(F32).reshape(DIFF_HEADS, 1, DIFF_DV))


def _const_spec(shape):
    zeros = (0,) * len(shape)
    return pl.BlockSpec(shape, lambda *_: zeros, pipeline_mode=pl.Buffered(1))


def _merge_kernel(x_ref, mod_ref, nw_ref, ya_ref, yb_ref, yc_ref, yd_ref, wg_ref, bg_ref, wb_ref, wo_ref,
                  o_ref):
    x = x_ref[...]
    h = _modulate(x, nw_ref[...], mod_ref[0, 0:1, :], mod_ref[0, 1:2, :]).astype(BF16)
    merged = None
    for n, y_ref in enumerate((ya_ref, yb_ref, yc_ref, yd_ref)):
        cols = slice(n * D_MODEL, (n + 1) * D_MODEL)
        gate = jax.nn.sigmoid(_dot(h, wg_ref[:, cols]) + bg_ref[:, cols])
        term = gate * _dot(y_ref[...], wb_ref[n])
        merged = term if merged is None else merged + term
    o_ref[...] = x + mod_ref[0, 2:3, :] * _dot(merged.astype(BF16), wo_ref[...])


def _merge(x2, mod3, nw, ys, wg, bg, wb, wo, L):
    m = x2.shape[0]
    tm = 512
    per_seq = L // tm
    row = lambda i: (i, 0)
    return pl.pallas_call(
        _merge_kernel,
        grid=(m // tm,),
        in_specs=[pl.BlockSpec((tm, D_MODEL), row),
                  pl.BlockSpec((1, 6, D_MODEL), lambda i: (i // per_seq, 0, 0)),
                  _const_spec((1, D_MODEL))]
                 + [pl.BlockSpec((tm, BRANCH_WIDTH), row)] * 4
                 + [_const_spec((D_MODEL, 4 * D_MODEL)), _const_spec((1, 4 * D_MODEL)),
                    _const_spec((4, BRANCH_WIDTH, D_MODEL)), _const_spec((D_MODEL, D_MODEL))],
        out_specs=pl.BlockSpec((tm, D_MODEL), row),
        out_shape=jax.ShapeDtypeStruct((m, D_MODEL), F32),
        compiler_params=_cparams(("arbitrary",)),
        name="merge",
    )(x2, mod3, nw, *ys, wg, bg, wb, wo)


def _ffn_kernel(x_ref, mod_ref, nw_ref, w1_ref, w3_ref, w2_ref, fw_ref, o_ref, h_sc, acc_sc, *, final):
    j = pl.program_id(1)

    @pl.when(j == 0)
    def _():
        h_sc[...] = _modulate(x_ref[...], nw_ref[...], mod_ref[0, 3:4, :], mod_ref[0, 4:5, :]).astype(BF16)
        acc_sc[...] = jnp.zeros_like(acc_sc)

    h = h_sc[...]
    a = _dot(h, w1_ref[...])
    acc_sc[...] += _dot((_silu(a) * _dot(h, w3_ref[...])).astype(BF16), w2_ref[...])

    @pl.when(j == pl.num_programs(1) - 1)
    def _():
        y = x_ref[...] + mod_ref[0, 5:6, :] * acc_sc[...]
        if final:
            y = y * lax.rsqrt(jnp.mean(y * y, axis=-1, keepdims=True) + RMS_EPS) * fw_ref[...]
        o_ref[...] = y


def _ffn(x2, mod3, nw, w1, w3, w2, fw, L, final):
    m = x2.shape[0]
    tm, th = 512, 1408
    per_seq = L // tm
    return pl.pallas_call(
        functools.partial(_ffn_kernel, final=final),
        grid=(m // tm, FFN_DENSE // th),
        in_specs=[pl.BlockSpec((tm, D_MODEL), lambda i, j: (i, 0)),
                  pl.BlockSpec((1, 6, D_MODEL), lambda i, j: (i // per_seq, 0, 0)),
                  pl.BlockSpec((1, D_MODEL), lambda i, j: (0, 0)),
                  pl.BlockSpec((D_MODEL, th), lambda i, j: (0, j)),
                  pl.BlockSpec((D_MODEL, th), lambda i, j: (0, j)),
                  pl.BlockSpec((th, D_MODEL), lambda i, j: (j, 0)),
                  pl.BlockSpec((1, D_MODEL), lambda i, j: (0, 0))],
        out_specs=pl.BlockSpec((tm, D_MODEL), lambda i, j: (i, 0)),
        out_shape=jax.ShapeDtypeStruct((m, D_MODEL), F32),
        scratch_shapes=[pltpu.VMEM((tm, D_MODEL), BF16), pltpu.VMEM((tm, D_MODEL), F32)],
        compiler_params=_cparams(("arbitrary", "arbitrary")),
        name="ffn",
    )(x2, mod3, nw, w1, w3, w2, fw)


MOE_TM = 512
MOE_BLK = 512
MOE_TH = 512
MOE_NBL = 2 * N_EXPERTS
R_E0, R_E1, R_R0, R_R1, R_G0, R_G1 = range(6)
M_D0, M_D1, M_G0, M_G1 = range(4)


def _router_kernel(x_ref, mod_ref, nw_ref, wr_ref, tri_ref, h_ref, route_ref, run_sc):
    tm = MOE_TM

    @pl.when(pl.program_id(0) == 0)
    def _():
        run_sc[...] = jnp.zeros_like(run_sc)

    h2 = _modulate(x_ref[...], nw_ref[...], mod_ref[0, 3:4, :], mod_ref[0, 4:5, :])
    hi, lo = _split2(h2)
    h_ref[...] = hi
    logits = _dot(hi, wr_ref[0]) + _dot(lo, wr_ref[0]) + _dot(hi, wr_ref[1])
    lane = lax.broadcasted_iota(jnp.int32, (tm, LANES), 1).astype(F32)
    logits = jnp.where(lane < N_EXPERTS, logits, NEG_BIG)
    m1 = jnp.max(logits, axis=-1, keepdims=True)
    i1 = jnp.min(jnp.where(logits == m1, lane, float(LANES)), axis=-1, keepdims=True)
    oh1 = lane == i1
    rest = jnp.where(oh1, NEG_BIG, logits)
    m2 = jnp.max(rest, axis=-1, keepdims=True)
    i2 = jnp.min(jnp.where(rest == m2, lane, float(LANES)), axis=-1, keepdims=True)
    oh2 = lane == i2
    e = jnp.exp(m2 - m1)
    g1 = 1.0 / (1.0 + e)
    g2 = e * g1
    oh = jnp.where(oh1, 1.0, 0.0) + jnp.where(oh2, 1.0, 0.0)
    before = _dot(tri_ref[...], oh.astype(BF16)) + run_sc[...]
    r1 = jnp.sum(jnp.where(oh1, before, 0.0), axis=-1, keepdims=True)
    r2 = jnp.sum(jnp.where(oh2, before, 0.0), axis=-1, keepdims=True)
    run_sc[...] += jnp.sum(oh, axis=0, keepdims=True)
    rec = jnp.zeros((tm, LANES), F32)
    for pos, val in ((R_E0, i1), (R_E1, i2), (R_R0, r1), (R_R1, r2), (R_G0, g1), (R_G1, g2)):
        rec = jnp.where(lane == float(pos), val, rec)
    route_ref[...] = rec


def _expert_kernel(be_ref, nu_ref, tlo_ref, thi_ref, h_hbm, meta_hbm, w1_ref, w3_ref, w2_ref, o_ref,
                   x_sc, rg_sc, acc_sc, hbuf, mbuf, sem):
    del be_ref
    i = pl.program_id(0)
    j = pl.program_id(1)
    last = pl.num_programs(1) - 1
    used = i < nu_ref[0]

    def fetch(tt, slot):
        return (pltpu.make_async_copy(h_hbm.at[tt], hbuf.at[slot], sem.at[0, slot]),
                pltpu.make_async_copy(meta_hbm.at[tt], mbuf.at[slot], sem.at[1, slot]))

    @pl.when(jnp.logical_and(used, j == 0))
    def _():
        tlo = tlo_ref[i]
        thi = thi_ref[i]
        rows = (lax.broadcasted_iota(jnp.int32, (MOE_BLK, 1), 0) + i * MOE_BLK).astype(F32)
        acc_sc[...] = jnp.zeros_like(acc_sc)
        rg_sc[...] = jnp.zeros_like(rg_sc)

        @pl.when(tlo <= thi)
        def _():
            for cp in fetch(tlo, 0):
                cp.start()

        def body(tt, carry):
            slot = (tt - tlo) % 2
            for cp in fetch(tt, slot):
                cp.wait()

            @pl.when(tt < thi)
            def _():
                for cp in fetch(tt + 1, 1 - slot):
                    cp.start()

            meta = mbuf[slot]
            p0 = rows == meta[M_D0:M_D0 + 1, :]
            p1 = rows == meta[M_D1:M_D1 + 1, :]
            sel = jnp.where(p0, 1.0, jnp.where(p1, 1.0, 0.0)).astype(BF16)
            acc_sc[...] += _dot(sel, hbuf[slot])
            rg_sc[...] += jnp.sum(jnp.where(p0, meta[M_G0:M_G0 + 1, :], 0.0)
                                  + jnp.where(p1, meta[M_G1:M_G1 + 1, :], 0.0), axis=1, keepdims=True)
            return carry

        lax.fori_loop(tlo, thi + 1, body, 0)
        x_sc[...] = acc_sc[...].astype(BF16)
        acc_sc[...] = jnp.zeros_like(acc_sc)

    @pl.when(used)
    def _():
        x = x_sc[...]
        a = _dot(x, w1_ref[0])
        acc_sc[...] += _dot((_silu(a) * _dot(x, w3_ref[0])).astype(BF16), w2_ref[0])

        @pl.when(j == last)
        def _():
            o_ref[0] = (acc_sc[...] * rg_sc[...]).astype(BF16)

    @pl.when(jnp.logical_and(jnp.logical_not(used), j == last))
    def _():
        o_ref[...] = jnp.zeros_like(o_ref)


def _combine_kernel(bl_ref, x_ref, dc_ref, mod_ref, fw_ref, y_hbm, o_ref, f_sc, ybuf, sem, *, final):
    i = pl.program_id(0)

    def fetch(n):
        return pltpu.make_async_copy(y_hbm.at[bl_ref[i * MOE_NBL + n]], ybuf.at[n], sem.at[n])

    for n in range(MOE_NBL):
        @pl.when(bl_ref[i * MOE_NBL + n] >= 0)
        def _():
            fetch(n).start()

    f_sc[...] = jnp.zeros_like(f_sc)
    d0 = dc_ref[:, 0:1]
    d1 = dc_ref[:, 1:2]
    lanes = lax.broadcasted_iota(jnp.int32, (1, MOE_BLK), 1)
    for n in range(MOE_NBL):
        blk = bl_ref[i * MOE_NBL + n]

        @pl.when(blk >= 0)
        def _():
            fetch(n).wait()
            r = (lanes + blk * MOE_BLK).astype(F32)
            sel = jnp.where(d0 == r, 1.0, jnp.where(d1 == r, 1.0, 0.0)).astype(BF16)
            f_sc[...] += _dot(sel, ybuf[n])

    y = x_ref[...] + mod_ref[0, 5:6, :] * f_sc[...]
    if final:
        y = y * lax.rsqrt(jnp.mean(y * y, axis=-1, keepdims=True) + RMS_EPS) * fw_ref[...]
    o_ref[...] = y


def _moe(x2, mod3, nw, router, w1, w3, w2, fw, L, final):
    t = x2.shape[0]
    tm = MOE_TM
    per_seq = L // tm
    r_hi = jnp.zeros((D_MODEL, LANES), F32).at[:, :N_EXPERTS].set(router.astype(F32))
    r_split = jnp.stack(_split2(r_hi))
    ii = np.arange(tm)
    tri = jnp.asarray(ii[:, None] > ii[None, :], BF16)
    h2, route = pl.pallas_call(
        _router_kernel,
        grid=(t // tm,),
        in_specs=[pl.BlockSpec((tm, D_MODEL), lambda i: (i, 0)),
                  pl.BlockSpec((1, 6, D_MODEL), lambda i: (i // per_seq, 0, 0)),
                  pl.BlockSpec((1, D_MODEL), lambda i: (0, 0)),
                  pl.BlockSpec((2, D_MODEL, LANES), lambda i: (0, 0, 0)),
                  pl.BlockSpec((tm, tm), lambda i: (0, 0))],
        out_specs=[pl.BlockSpec((tm, D_MODEL), lambda i: (i, 0)),
                   pl.BlockSpec((tm, LANES), lambda i: (i, 0))],
        out_shape=[jax.ShapeDtypeStruct((t, D_MODEL), BF16),
                   jax.ShapeDtypeStruct((t, LANES), F32)],
        scratch_shapes=[pltpu.VMEM((1, LANES), F32)],
        compiler_params=_cparams(("arbitrary",)),
        name="router",
    )(x2, mod3, nw, r_split, tri)

    n_tiles = t // tm
    n_assign = t * TOP_K
    e = route[:, R_E0:R_E1 + 1].astype(jnp.int32)
    rank = route[:, R_R0:R_R1 + 1].astype(jnp.int32)
    gate = route[:, R_G0:R_G1 + 1]
    tile_cnt = jnp.sum(jax.nn.one_hot(e, N_EXPERTS, dtype=jnp.int32).reshape(n_tiles, tm * TOP_K, N_EXPERTS),
                       axis=1)
    cum = jnp.cumsum(tile_cnt, axis=0)
    cntb = jnp.concatenate([jnp.zeros((1, N_EXPERTS), jnp.int32), cum], axis=0)
    counts = cum[-1]
    padded = (counts + MOE_BLK - 1) // MOE_BLK * MOE_BLK
    pad_ends = jnp.cumsum(padded)
    pad_starts = pad_ends - padded
    dest = pad_starts[e] + rank
    n_blocks = n_assign // MOE_BLK + N_EXPERTS
    block_start = jnp.arange(n_blocks, dtype=jnp.int32) * MOE_BLK
    block_expert = jnp.minimum(jnp.sum(pad_ends[None, :] <= block_start[:, None], axis=1),
                               N_EXPERTS - 1).astype(jnp.int32)
    n_used = (pad_ends[-1:] // MOE_BLK).astype(jnp.int32)
    r_lo = block_start - pad_starts[block_expert]
    r_hi = jnp.minimum(r_lo + MOE_BLK, counts[block_expert])
    cb = cntb[:, block_expert]
    tile_lo = jnp.sum(cb[1:] <= r_lo[None, :], axis=0).astype(jnp.int32)
    tile_hi = (jnp.sum(cb[:-1] < r_hi[None, :], axis=0) - 1).astype(jnp.int32)
    s_lo = pad_starts[None, :] + cntb[:-1]
    s_hi = pad_starts[None, :] + cntb[1:]
    has = s_hi > s_lo
    b0 = s_lo // MOE_BLK
    b1 = (s_hi - 1) // MOE_BLK
    blist = jnp.stack([jnp.where(has, b0, -1), jnp.where(has & (b1 != b0), b1, -1)], axis=-1)
    blist = blist.reshape(n_tiles * MOE_NBL).astype(jnp.int32)
    destf = dest.astype(F32)
    meta = jnp.zeros((8, t), F32).at[M_D0].set(destf[:, 0]).at[M_D1].set(destf[:, 1])
    meta = meta.at[M_G0].set(gate[:, 0]).at[M_G1].set(gate[:, 1])
    meta = jnp.transpose(meta.reshape(8, n_tiles, tm), (1, 0, 2))
    dcol = jnp.zeros((t, 8), F32).at[:, :TOP_K].set(destf)

    nj = FFN_EXPERT // MOE_TH

    def jcol(i, j, nu):
        return jnp.where(i < nu[0], j, 0)

    y_rows = pl.pallas_call(
        _expert_kernel,
        grid_spec=pltpu.PrefetchScalarGridSpec(
            num_scalar_prefetch=4,
            grid=(n_blocks, nj),
            in_specs=[pl.BlockSpec(memory_space=pl.ANY),
                      pl.BlockSpec(memory_space=pl.ANY),
                      pl.BlockSpec((1, D_MODEL, MOE_TH), lambda i, j, be, nu, lo, hi: (be[i], 0, jcol(i, j, nu))),
                      pl.BlockSpec((1, D_MODEL, MOE_TH), lambda i, j, be, nu, lo, hi: (be[i], 0, jcol(i, j, nu))),
                      pl.BlockSpec((1, MOE_TH, D_MODEL), lambda i, j, be, nu, lo, hi: (be[i], jcol(i, j, nu), 0))],
            out_specs=pl.BlockSpec((1, MOE_BLK, D_MODEL), lambda i, j, be, nu, lo, hi: (i, 0, 0)),
            scratch_shapes=[pltpu.VMEM((MOE_BLK, D_MODEL), BF16),
                            pltpu.VMEM((MOE_BLK, 1), F32),
                            pltpu.VMEM((MOE_BLK, D_MODEL), F32),
                            pltpu.VMEM((2, tm, D_MODEL), BF16),
                            pltpu.VMEM((2, 8, tm), F32),
                            pltpu.SemaphoreType.DMA((2, 2))]),
        out_shape=jax.ShapeDtypeStruct((n_blocks, MOE_BLK, D_MODEL), BF16),
        compiler_params=_cparams(("arbitrary", "arbitrary")),
        name="experts",
    )(block_expert, n_used, tile_lo, tile_hi, h2.reshape(n_tiles, tm, D_MODEL), meta, w1, w3, w2)

    return pl.pallas_call(
        functools.partial(_combine_kernel, final=final),
        grid_spec=pltpu.PrefetchScalarGridSpec(
            num_scalar_prefetch=1,
            grid=(n_tiles,),
            in_specs=[pl.BlockSpec((tm, D_MODEL), lambda i, bl: (i, 0)),
                      pl.BlockSpec((tm, 8), lambda i, bl: (i, 0)),
                      pl.BlockSpec((1, 6, D_MODEL), lambda i, bl: (i // per_seq, 0, 0)),
                      pl.BlockSpec((1, D_MODEL), lambda i, bl: (0, 0)),
                      pl.BlockSpec(memory_space=pl.ANY)],
            out_specs=pl.BlockSpec((tm, D_MODEL), lambda i, bl: (i, 0)),
            scratch_shapes=[pltpu.VMEM((tm, D_MODEL), F32),
                            pltpu.VMEM((MOE_NBL, MOE_BLK, D_MODEL), BF16),
                            pltpu.SemaphoreType.DMA((MOE_NBL,))]),
        out_shape=jax.ShapeDtypeStruct((t, D_MODEL), F32),
        compiler_params=_cparams(("arbitrary",)),
        name="combine",
    )(blist, x2, dcol, mod3, fw, y_rows)


_O_SX, _O_SBC, _O_SDT, _O_CQ, _O_CK, _O_CV, _O_DQ = 2048, 2560, 2816, 2832, 3344, 3472, 3600


def _swa_pair_heads(w, axis):
    shp = w.shape
    w = w.reshape(shp[:axis] + (2, SWA_HEADS // 2, SWA_HEAD_DIM) + shp[axis + 1:])
    return jnp.swapaxes(w, axis, axis + 1).reshape(shp)


def _prep_layer(l, p):
    w_in = p['w_in'][l]
    w_u = jnp.concatenate([w_in[:, :_O_SX + 512], _swa_pair_heads(w_in[:, _O_CQ:_O_CQ + 512], 1),
                           w_in[:, _O_DQ:_O_DQ + 1536], w_in[:, _O_SBC:_O_SBC + 256],
                           w_in[:, _O_CK:_O_CK + 128], w_in[:, _O_CV:_O_CV + 128]], axis=1)
    w = {
        'w_u': w_u.astype(BF16),
        'w_dt': jnp.zeros((D_MODEL, LANES), F32).at[:, :2 * SSD_HEADS].set(
            w_in[:, _O_SDT:_O_SDT + 2 * SSD_HEADS]).astype(BF16),
        'wg': p['w_gate'][l].astype(BF16),
        'bg': p['b_gate'][l].astype(F32).reshape(1, -1),
        'wb': p['w_branch'][l].at[2].set(_swa_pair_heads(p['w_branch'][l][2], 0)).astype(BF16),
        'wo': p['w_out'][l].astype(BF16),
    }
    i = l // 2
    if l % 2 == 0:
        w['ffn'] = (p['ffn_w1'][i].astype(BF16), p['ffn_w3'][i].astype(BF16), p['ffn_w2'][i].astype(BF16))
    else:
        w['moe'] = (p['moe_router'][i], p['moe_w1'][i].astype(BF16), p['moe_w3'][i].astype(BF16),
                    p['moe_w2'][i].astype(BF16))
    return w


def _trunk(x, c, p, layers):
    b, L, _ = x.shape
    mod = _ada(c, p['w_ada'], p['b_ada'])
    x2 = x.reshape(b * L, D_MODEL)
    fw = p['final_norm_w'].astype(F32).reshape(1, D_MODEL)
    for l in range(DEPTH):
        w = layers[l]
        final = l == DEPTH - 1
        u, dt = _inproj(x2, mod[l], p['norm_mix_w'][l].reshape(1, D_MODEL), w['w_u'], w['w_dt'], L)
        ya = _retention(u, b, L, p['ret_decay_logit'][l], p['ret_norm_w'][l])
        yb = _ssd(u, dt, b, L, p['ssd_conv_w'][l], p['ssd_conv_b'][l], p['ssd_dt_bias'][l],
                  p['ssd_a_log'][l], p['ssd_d'][l], p['ssd_norm_w'][l])
        yc = _swa(u, b, L, p['swa_sink'][l], p['rel_bias'][:, :SWA_HEADS])
        yd = _diff(u, b, L, p['diff_lambda'][l], 0.8 - 0.6 * math.exp(-0.3 * l), p['diff_norm_w'][l],
                   p['rel_bias'][:, SWA_HEADS:])
        x2 = _merge(x2, mod[l], p['norm_mix_w'][l].reshape(1, D_MODEL), (ya, yb, yc, yd),
                    w['wg'], w['bg'], w['wb'], w['wo'], L)
        nw = p['norm_ffn_w'][l].reshape(1, D_MODEL)
        if l % 2 == 0:
            x2 = _ffn(x2, mod[l], nw, *w['ffn'], fw, L, final)
        else:
            x2 = _moe(x2, mod[l], nw, *w['moe'], fw, L, final)
    return x2.reshape(b, L, D_MODEL)


def kernel(x_prompt, x_sample, c_prompt, c_sample, norm_mix_w, norm_ffn_w, w_ada, b_ada, w_in,
           ret_decay_logit, ret_norm_w, ssd_conv_w, ssd_conv_b, ssd_dt_bias, ssd_a_log, ssd_d,
           ssd_norm_w, swa_sink, diff_lambda, diff_norm_w, rel_bias, w_gate, b_gate, w_branch, w_out,
           ffn_w1, ffn_w3, ffn_w2, moe_router, moe_w1, moe_w3, moe_w2, final_norm_w):
    p = {'norm_mix_w': norm_mix_w, 'norm_ffn_w': norm_ffn_w, 'w_ada': w_ada, 'b_ada': b_ada,
         'w_in': w_in, 'ret_decay_logit': ret_decay_logit, 'ret_norm_w': ret_norm_w,
         'ssd_conv_w': ssd_conv_w, 'ssd_conv_b': ssd_conv_b, 'ssd_dt_bias': ssd_dt_bias,
         'ssd_a_log': ssd_a_log, 'ssd_d': ssd_d, 'ssd_norm_w': ssd_norm_w, 'swa_sink': swa_sink,
         'diff_lambda': diff_lambda, 'diff_norm_w': diff_norm_w, 'rel_bias': rel_bias,
         'w_gate': w_gate, 'b_gate': b_gate, 'w_branch': w_branch, 'w_out': w_out,
         'ffn_w1': ffn_w1, 'ffn_w3': ffn_w3, 'ffn_w2': ffn_w2, 'moe_router': moe_router,
         'moe_w1': moe_w1, 'moe_w3': moe_w3, 'moe_w2': moe_w2, 'final_norm_w': final_norm_w}
    layers = [_prep_layer(l, p) for l in range(DEPTH)]
    return (_trunk(x_prompt, c_prompt, p, layers), _trunk(x_sample, c_sample, p, layers))
```
